```python
import math
import numpy as np
import jax
import jax.numpy as jnp
from jax import lax

D_MODEL = 1024
BATCH = 8
SEQ = 4096
DEPTH = 2

HEAD_DIM = 64
BLOCK = 128
EPS = 1e-6
A_Q_HEADS = 8
A_KV_HEADS = 2
A_GROUP = A_Q_HEADS // A_KV_HEADS
A_WIDTH = A_Q_HEADS * HEAD_DIM
WINDOW = 128
B_WIDTH = D_MODEL // 2
CONV_WIDTH = 31
C_HEADS = 4
C_WIDTH = C_HEADS * 2 * HEAD_DIM
N_BRANCH = 3

A_Q_COLS = A_WIDTH
A_K_COLS = A_KV_HEADS * HEAD_DIM
A_V_COLS = A_KV_HEADS * HEAD_DIM
A_G_COLS = A_WIDTH
B_GLU_COLS = 2 * B_WIDTH
B_G_COLS = B_WIDTH
C_Q_COLS = C_WIDTH
C_K_COLS = C_WIDTH
C_V_COLS = C_WIDTH
C_G_COLS = C_WIDTH
MERGE_COLS = N_BRANCH * D_MODEL
SPLITS = (A_Q_COLS, A_K_COLS, A_V_COLS, A_G_COLS, B_GLU_COLS, B_G_COLS,
          C_Q_COLS, C_K_COLS, C_V_COLS, C_G_COLS, MERGE_COLS)
IN_WIDTH = sum(SPLITS)

kernel_name = "hybrid_swa_sink_conformer_diffattn_gated"


def rms_norm(x, g):
    xf = x.astype(jnp.float32)
    y = xf * lax.rsqrt(jnp.mean(xf * xf, axis=-1, keepdims=True) + EPS)
    return (y * g.astype(jnp.float32)).astype(x.dtype)


def sliding_window_attention(q, k, v, sinks):
    B, T, _, d = q.shape
    nb = T // BLOCK
    qb = q.reshape(B, nb, BLOCK, A_KV_HEADS, A_GROUP, d)

    def band(t):
        tb = t.reshape(B, nb, BLOCK, A_KV_HEADS, d)
        prev = jnp.concatenate([jnp.zeros_like(tb[:, :1]), tb[:, :-1]], axis=1)
        return jnp.concatenate([prev, tb], axis=2)

    kw, vw = band(k), band(v)
    s = jnp.einsum('bnqkgd,bnskd->bnkgqs', qb, kw,
                   preferred_element_type=jnp.float32) * (d ** -0.5)
    qi = jnp.arange(BLOCK)[:, None] + BLOCK
    kj = jnp.arange(2 * BLOCK)[None, :]
    rel = qi - kj
    local = (rel >= 0) & (rel < WINDOW)
    kabs = jnp.arange(nb)[:, None, None] * BLOCK - BLOCK + kj[None]
    mask = local[None] & (kabs >= 0)
    s = jnp.where(mask[None, :, None, None], s, -jnp.inf)
    sink = sinks.astype(jnp.float32).reshape(1, 1, A_KV_HEADS, A_GROUP, 1, 1)
    m = jnp.maximum(jnp.max(s, axis=-1, keepdims=True), sink)
    p = jnp.exp(s - m)
    p = p / (jnp.sum(p, axis=-1, keepdims=True) + jnp.exp(sink - m))
    o = jnp.einsum('bnkgqs,bnskd->bnqkgd', p.astype(v.dtype), vw)
    return o.reshape(B, T, A_Q_HEADS * d)


def conformer_conv(glu_in, conv_w, conv_b, ln_g, ln_b):
    a, b = jnp.split(glu_in, 2, axis=-1)
    u = a * jax.nn.sigmoid(b)
    u = lax.conv_general_dilated(
        u, conv_w[:, None, :], window_strides=(1,),
        padding=[(CONV_WIDTH - 1, 0)],
        dimension_numbers=('NWC', 'WIO', 'NWC'),
        feature_group_count=B_WIDTH) + conv_b
    uf = u.astype(jnp.float32)
    mu = jnp.mean(uf, axis=-1, keepdims=True)
    var = jnp.mean(jnp.square(uf - mu), axis=-1, keepdims=True)
    un = (uf - mu) * lax.rsqrt(var + EPS) * ln_g.astype(jnp.float32) + ln_b.astype(jnp.float32)
    return jax.nn.silu(un).astype(u.dtype)


def differential_attention(q, k, v, lam, lam_init, subln_g):
    B, T, H, _, d = q.shape
    nb = T // BLOCK
    qb = jnp.moveaxis(q.reshape(B, nb, BLOCK, H, 2, d), 1, 0)
    kpos = jnp.arange(T)

    def one_block(args):
        q_blk, n = args
        s = jnp.einsum('bqhcd,bshcd->bhcqs', q_blk, k,
                       preferred_element_type=jnp.float32) * (d ** -0.5)
        qpos = n * BLOCK + jnp.arange(BLOCK)
        causal = kpos[None, :] <= qpos[:, None]
        a = jax.nn.softmax(jnp.where(causal, s, -jnp.inf), axis=-1)
        diff = a[:, :, 0] - lam * a[:, :, 1]
        return jnp.einsum('bhqs,bshe->bqhe', diff.astype(v.dtype), v)

    o = lax.map(one_block, (qb, jnp.arange(nb)))
    o = jnp.moveaxis(o, 0, 1).reshape(B, T, H, 2 * d)
    o = rms_norm(o, subln_g) * (1.0 - lam_init)
    return o.reshape(B, T, H * 2 * d)


def setup_inputs(seed: int = 0) -> dict:
    key = jax.random.key(seed)
    ks = jax.random.split(key, 24)
    f = jnp.float32
    nrm = lambda k, shape, s: jax.random.normal(k, shape, f) * s
    L, D = DEPTH, D_MODEL
    return {
        "x": nrm(ks[0], (BATCH, SEQ, D), 1.0),
        "norm_g": 1.0 + nrm(ks[1], (L, D), 0.02),
        "w_in": nrm(ks[2], (L, D, IN_WIDTH), D ** -0.5),
        "attn_q_norm_g": 1.0 + nrm(ks[3], (L, HEAD_DIM), 0.02),
        "attn_k_norm_g": 1.0 + nrm(ks[4], (L, HEAD_DIM), 0.02),
        "attn_sinks": nrm(ks[5], (L, A_Q_HEADS), 0.5),
        "w_o_attn": nrm(ks[6], (L, A_WIDTH, D), A_WIDTH ** -0.5),
        "conv_w": nrm(ks[7], (L, CONV_WIDTH, B_WIDTH), CONV_WIDTH ** -0.5),
        "conv_b": nrm(ks[8], (L, B_WIDTH), 0.02),
        "conv_norm_g": 1.0 + nrm(ks[9], (L, B_WIDTH), 0.02),
        "conv_norm_b": nrm(ks[10], (L, B_WIDTH), 0.02),
        "w_o_conv": nrm(ks[11], (L, B_WIDTH, D), B_WIDTH ** -0.5),
        "diff_q_norm_g": 1.0 + nrm(ks[12], (L, HEAD_DIM), 0.02),
        "diff_k_norm_g": 1.0 + nrm(ks[13], (L, HEAD_DIM), 0.02),
        "lambda_q": nrm(ks[14], (L, 2, HEAD_DIM), 0.1),
        "lambda_k": nrm(ks[15], (L, 2, HEAD_DIM), 0.1),
        "diff_subln_g": 1.0 + nrm(ks[16], (L, 2 * HEAD_DIM), 0.02),
        "w_o_diff": nrm(ks[17], (L, C_WIDTH, D), C_WIDTH ** -0.5),
        "w_out": nrm(ks[18], (L, D, D), D ** -0.5),
    }


def reference(x, norm_g, w_in, attn_q_norm_g, attn_k_norm_g, attn_sinks, w_o_attn,
              conv_w, conv_b, conv_norm_g, conv_norm_b, w_o_conv,
              diff_q_norm_g, diff_k_norm_g, lambda_q, lambda_k, diff_subln_g, w_o_diff,
              w_out):
    B, T, D = x.shape
    split_points = [int(s) for s in np.cumsum(SPLITS)[:-1]]
    for l in range(DEPTH):
        h = rms_norm(x, norm_g[l])
        z = jnp.einsum('btd,de->bte', h, w_in[l])
        aq, ak, av, ag, bglu, bg, cq, ck, cv, cg, mg = jnp.split(z, split_points, axis=-1)

        aq = rms_norm(aq.reshape(B, T, A_Q_HEADS, HEAD_DIM), attn_q_norm_g[l])
        ak = rms_norm(ak.reshape(B, T, A_KV_HEADS, HEAD_DIM), attn_k_norm_g[l])
        av = av.reshape(B, T, A_KV_HEADS, HEAD_DIM)
        ya = sliding_window_attention(aq, ak, av, attn_sinks[l]) * jax.nn.silu(ag)
        ya = ya @ w_o_attn[l]

        yb = conformer_conv(bglu, conv_w[l], conv_b[l], conv_norm_g[l], conv_norm_b[l])
        yb = (yb * jax.nn.silu(bg)) @ w_o_conv[l]

        lam_init = 0.8 - 0.6 * math.exp(-0.3 * l)
        e = jnp.exp(jnp.sum(lambda_q[l].astype(jnp.float32) * lambda_k[l].astype(jnp.float32), axis=-1))
        lam = e[0] - e[1] + lam_init
        cq = rms_norm(cq.reshape(B, T, C_HEADS, 2, HEAD_DIM), diff_q_norm_g[l])
        ck = rms_norm(ck.reshape(B, T, C_HEADS, 2, HEAD_DIM), diff_k_norm_g[l])
        cv = cv.reshape(B, T, C_HEADS, 2 * HEAD_DIM)
        yc = differential_attention(cq, ck, cv, lam, lam_init, diff_subln_g[l]) * jax.nn.silu(cg)
        yc = yc @ w_o_diff[l]

        gates = jax.nn.sigmoid(mg).reshape(B, T, N_BRANCH, D)
        merged = gates[:, :, 0] * ya + gates[:, :, 1] * yb + gates[:, :, 2] * yc
        x = x + merged @ w_out[l]
    return x
```

```python
import functools
import math

import jax
import jax.numpy as jnp
from jax import lax
from jax.experimental import pallas as pl
from jax.experimental.pallas import tpu as pltpu

F32 = jnp.float32
BF16 = jnp.bfloat16

D_MODEL = 1024
HEAD_DIM = 64
EPS = 1e-6
A_Q_HEADS = 8
A_KV_HEADS = 2
A_GROUP = A_Q_HEADS // A_KV_HEADS
A_WIDTH = A_Q_HEADS * HEAD_DIM
A_KV_WIDTH = A_KV_HEADS * HEAD_DIM
WINDOW = 128
B_WIDTH = D_MODEL // 2
CONV_WIDTH = 31
C_HEADS = 4
C_HEAD_WIDTH = 2 * HEAD_DIM
C_WIDTH = C_HEADS * C_HEAD_WIDTH
N_BRANCH = 3
MERGE_COLS = N_BRANCH * D_MODEL

OFF_AQ = 0
OFF_AK = OFF_AQ + A_WIDTH
OFF_AV = OFF_AK + A_KV_WIDTH
OFF_AG = OFF_AV + A_KV_WIDTH
OFF_BGLU = OFF_AG + A_WIDTH
OFF_BG = OFF_BGLU + 2 * B_WIDTH
OFF_CQ = OFF_BG + B_WIDTH
OFF_CK = OFF_CQ + C_WIDTH
OFF_CV = OFF_CK + C_WIDTH
OFF_CG = OFF_CV + C_WIDTH
OFF_MG = OFF_CG + C_WIDTH
IN_WIDTH = OFF_MG + MERGE_COLS

MXU_COLS = 256
LANES = 128
VMEM_LIMIT_BYTES = 56 * 1024 * 1024

ROW_TILE = 512
SWA_TILE = 512
CONV_TILE = 256
CONV_HALO = 32
CONV_CHUNK = 64
DIFF_TQ = 256
DIFF_TK = 256


def _sigmoid(y):
    return 1.0 / (1.0 + jnp.exp(-y))


def _silu(y):
    return y * _sigmoid(y)


def _params(semantics):
    return pltpu.CompilerParams(dimension_semantics=semantics, vmem_limit_bytes=VMEM_LIMIT_BYTES)


def _inproj_kernel(x_ref, g_ref, w_ref, bd_ref, aqg_ref, akg_ref, cqg_ref, ckg_ref,
                   aq_o, ak_o, av_o, ag_o, u_o, bg_o, cq_o, ck_o, cv_o, cg_o, mg_o):
    x = x_ref[...]
    ms = jnp.mean(x * x, axis=-1, keepdims=True)
    h = (x * lax.rsqrt(ms + EPS) * g_ref[...]).astype(BF16)

    def proj(off, width):
        return jnp.dot(h, w_ref[:, off:off + width], preferred_element_type=F32)

    def head_norm(y, gain, scale):
        w = y.shape[-1]
        sq = (y * y).astype(BF16)
        ssum = jnp.dot(sq, bd_ref[:w, :w], preferred_element_type=F32)
        return y * lax.rsqrt(ssum * (1.0 / HEAD_DIM) + EPS) * (gain * scale)

    qscale = HEAD_DIM ** -0.5
    for c in range(A_WIDTH // MXU_COLS):
        sl = slice(c * MXU_COLS, (c + 1) * MXU_COLS)
        aq_o[:, sl] = head_norm(proj(OFF_AQ + c * MXU_COLS, MXU_COLS), aqg_ref[...], qscale).astype(BF16)
        ag_o[:, sl] = _silu(proj(OFF_AG + c * MXU_COLS, MXU_COLS)).astype(BF16)
    ak_o[...] = head_norm(proj(OFF_AK, A_KV_WIDTH), akg_ref[:, :A_KV_WIDTH], 1.0).astype(BF16)
    av_o[...] = proj(OFF_AV, A_KV_WIDTH).astype(BF16)
    for c in range(B_WIDTH // MXU_COLS):
        sl = slice(c * MXU_COLS, (c + 1) * MXU_COLS)
        a = proj(OFF_BGLU + c * MXU_COLS, MXU_COLS)
        b = proj(OFF_BGLU + B_WIDTH + c * MXU_COLS, MXU_COLS)
        u_o[:, sl] = (a * _sigmoid(b)).astype(BF16)
        bg_o[:, sl] = _silu(proj(OFF_BG + c * MXU_COLS, MXU_COLS)).astype(BF16)
    for c in range(C_WIDTH // MXU_COLS):
        sl = slice(c * MXU_COLS, (c + 1) * MXU_COLS)
        cq_o[:, sl] = head_norm(proj(OFF_CQ + c * MXU_COLS, MXU_COLS), cqg_ref[...], qscale).astype(BF16)
        ck_o[:, sl] = head_norm(proj(OFF_CK + c * MXU_COLS, MXU_COLS), ckg_ref[...], 1.0).astype(BF16)
        cv_o[:, sl] = proj(OFF_CV + c * MXU_COLS, MXU_COLS).astype(BF16)
        cg_o[:, sl] = _silu(proj(OFF_CG + c * MXU_COLS, MXU_COLS)).astype(BF16)
    for c in range(MERGE_COLS // MXU_COLS):
        sl = slice(c * MXU_COLS, (c + 1) * MXU_COLS)
        mg_o[:, sl] = _sigmoid(proj(OFF_MG + c * MXU_COLS, MXU_COLS)).astype(BF16)


def _inproj(x2, norm_g, w_in, bd, aqg, akg, cqg, ckg):
    n = x2.shape[0]
    tm = min(ROW_TILE, n)
    widths = (A_WIDTH, A_KV_WIDTH, A_KV_WIDTH, A_WIDTH, B_WIDTH, B_WIDTH,
              C_WIDTH, C_WIDTH, C_WIDTH, C_WIDTH, MERGE_COLS)
    const = lambda i: (0, 0)
    row = lambda i: (i, 0)
    return pl.pallas_call(
        _inproj_kernel,
        grid=(n // tm,),
        in_specs=[
            pl.BlockSpec((tm, D_MODEL), row),
            pl.BlockSpec((1, D_MODEL), const),
            pl.BlockSpec((D_MODEL, IN_WIDTH), const, pipeline_mode=pl.Buffered(1)),
            pl.BlockSpec((MXU_COLS, MXU_COLS), const),
            pl.BlockSpec((1, MXU_COLS), const),
            pl.BlockSpec((1, MXU_COLS), const),
            pl.BlockSpec((1, MXU_COLS), const),
            pl.BlockSpec((1, MXU_COLS), const),
        ],
        out_specs=[pl.BlockSpec((tm, w), row) for w in widths],
        out_shape=[jax.ShapeDtypeStruct((n, w), BF16) for w in widths],
        compiler_params=_params(("parallel",)),
        name="inproj",
    )(x2, norm_g, w_in, bd, aqg, akg, cqg, ckg)


def _swa_kernel(sink_ref, q_ref, kc_ref, kp_ref, vc_ref, vp_ref, ag_ref, o_ref):
    i = pl.program_id(1)
    nblk = q_ref.shape[0] // WINDOW
    kfull = jnp.concatenate([kp_ref[...], kc_ref[...]], axis=0)
    vfull = jnp.concatenate([vp_ref[...], vc_ref[...]], axis=0)
    rows = A_GROUP * WINDOW
    t = lax.broadcasted_iota(jnp.int32, (rows, 2 * WINDOW), 0) & (WINDOW - 1)
    c = lax.broadcasted_iota(jnp.int32, (rows, 2 * WINDOW), 1)
    local = (c > t) & (c <= t + WINDOW)
    head_row = lax.broadcasted_iota(jnp.int32, (rows, 1), 0) >> int(math.log2(WINDOW))
    for j in range(nblk):
        q = q_ref[j * WINDOW:(j + 1) * WINDOW, :]
        kwin = kfull[j * WINDOW:(j + 2) * WINDOW, :]
        vwin = vfull[j * WINDOW:(j + 2) * WINDOW, :]
        if j == 0:
            mask = local & ((c >= WINDOW) | (i > 0))
        else:
            mask = local
        heads_out = []
        for kv in range(A_KV_HEADS):
            kh = kwin[:, kv * HEAD_DIM:(kv + 1) * HEAD_DIM]
            vh = vwin[:, kv * HEAD_DIM:(kv + 1) * HEAD_DIM]
            qs = jnp.concatenate(
                [q[:, (kv * A_GROUP + g) * HEAD_DIM:(kv * A_GROUP + g + 1) * HEAD_DIM] for g in range(A_GROUP)],
                axis=0)
            sink = jnp.zeros((rows, 1), F32)
            for g in range(A_GROUP):
                sink = jnp.where(head_row == g, sink_ref[kv * A_GROUP + g], sink)
            s = lax.dot_general(qs, kh, (((1,), (1,)), ((), ())), preferred_element_type=F32)
            s = jnp.where(mask, s, -jnp.inf)
            m = jnp.maximum(jnp.max(s, axis=-1, keepdims=True), sink)
            p = jnp.exp(s - m)
            denom = jnp.sum(p, axis=-1, keepdims=True) + jnp.exp(sink - m)
            p = p * (1.0 / denom)
            o = jnp.dot(p.astype(BF16), vh, preferred_element_type=F32)
            heads_out.extend(o[g * WINDOW:(g + 1) * WINDOW, :] for g in range(A_GROUP))
        o_all = jnp.concatenate(heads_out, axis=1)
        gate = ag_ref[j * WINDOW:(j + 1) * WINDOW, :].astype(F32)
        o_ref[j * WINDOW:(j + 1) * WINDOW, :] = (o_all * gate).astype(BF16)


def _swa(sinks, aq, ak, av, ag):
    b, t, _ = aq.shape
    tq = min(SWA_TILE, t)
    per = tq // WINDOW
    cur = lambda bi, i: (bi, i, 0)
    prev = lambda bi, i: (bi, jnp.maximum(i * per - 1, 0), 0)
    return pl.pallas_call(
        _swa_kernel,
        grid=(b, t // tq),
        in_specs=[
            pl.BlockSpec(memory_space=pltpu.SMEM),
            pl.BlockSpec((None, tq, A_WIDTH), cur),
            pl.BlockSpec((None, tq, A_KV_WIDTH), cur),
            pl.BlockSpec((None, WINDOW, A_KV_WIDTH), prev),
            pl.BlockSpec((None, tq, A_KV_WIDTH), cur),
            pl.BlockSpec((None, WINDOW, A_KV_WIDTH), prev),
            pl.BlockSpec((None, tq, A_WIDTH), cur),
        ],
        out_specs=pl.BlockSpec((None, tq, A_WIDTH), cur),
        out_shape=jax.ShapeDtypeStruct((b, t, A_WIDTH), BF16),
        compiler_params=_params(("parallel", "parallel")),
        name="swa",
    )(sinks, aq, ak, ak, av, av, ag)


def _conv_kernel(uc_ref, up_ref, w_ref, cb_ref, lg_ref, lb_ref, bg_ref, o_ref, ubuf):
    i = pl.program_id(1)
    tt = uc_ref.shape[0]
    halo = up_ref[...].astype(F32)
    ubuf[0:CONV_HALO, :] = jnp.where(i > 0, halo, 0.0)
    ubuf[CONV_HALO:, :] = uc_ref[...].astype(F32)
    first = CONV_HALO - (CONV_WIDTH - 1)
    for r0 in range(0, tt, CONV_CHUNK):
        acc = jnp.zeros((CONV_CHUNK, B_WIDTH), F32)
        for k in range(CONV_WIDTH):
            acc = acc + w_ref[k:k + 1, :] * ubuf[r0 + first + k:r0 + first + k + CONV_CHUNK, :]
        y = acc + cb_ref[...]
        mu = jnp.mean(y, axis=-1, keepdims=True)
        yc = y - mu
        var = jnp.mean(yc * yc, axis=-1, keepdims=True)
        yn = yc * lax.rsqrt(var + EPS) * lg_ref[...] + lb_ref[...]
        gate = bg_ref[r0:r0 + CONV_CHUNK, :].astype(F32)
        o_ref[r0:r0 + CONV_CHUNK, :] = (_silu(yn) * gate).astype(BF16)


def _conv(u, conv_w, conv_b, ln_g, ln_b, bg):
    b, t, _ = u.shape
    tt = min(CONV_TILE, t)
    per = tt // CONV_HALO
    cur = lambda bi, i: (bi, i, 0)
    prev = lambda bi, i: (bi, jnp.maximum(i * per - 1, 0), 0)
    const = lambda bi, i: (0, 0)
    return pl.pallas_call(
        _conv_kernel,
        grid=(b, t // tt),
        in_specs=[
            pl.BlockSpec((None, tt, B_WIDTH), cur),
            pl.BlockSpec((None, CONV_HALO, B_WIDTH), prev),
            pl.BlockSpec((CONV_WIDTH, B_WIDTH), const),
            pl.BlockSpec((1, B_WIDTH), const),
            pl.BlockSpec((1, B_WIDTH), const),
            pl.BlockSpec((1, B_WIDTH), const),
            pl.BlockSpec((None, tt, B_WIDTH), cur),
        ],
        out_specs=pl.BlockSpec((None, tt, B_WIDTH), cur),
        out_shape=jax.ShapeDtypeStruct((b, t, B_WIDTH), BF16),
        scratch_shapes=[pltpu.VMEM((tt + CONV_HALO, B_WIDTH), F32)],
        compiler_params=_params(("parallel", "parallel")),
        name="conv",
    )(u, u, conv_w, conv_b, ln_g, ln_b, bg)


def _diff_kernel(lq_ref, lk_ref, q_ref, k_ref, v_ref, cg_ref, sg_ref, o_ref, *, lam_init):
    i = pl.program_id(2)
    tq = q_ref.shape[0]
    tk = DIFF_TK
    e = jnp.exp(jnp.sum(lq_ref[...] * lk_ref[...], axis=-1, keepdims=True))
    lam = e[0:1, :] - e[1:2, :] + lam_init

    q = q_ref[...]
    lane = lax.broadcasted_iota(jnp.int32, q.shape, 1)
    zero = jnp.zeros_like(q)
    qs = jnp.concatenate([jnp.where(lane < HEAD_DIM, q, zero), jnp.where(lane >= HEAD_DIM, q, zero)], axis=0)

    def step(j, carry, masked):
        m, l, acc = carry
        start = pl.multiple_of(j * tk, tk)
        kt = k_ref[pl.ds(start, tk), :]
        vt = v_ref[pl.ds(start, tk), :]
        s = lax.dot_general(qs, kt, (((1,), (1,)), ((), ())), preferred_element_type=F32)
        if masked:
            qpos = lax.broadcasted_iota(jnp.int32, s.shape, 0) & (tq - 1)
            kpos = lax.broadcasted_iota(jnp.int32, s.shape, 1)
            s = jnp.where(kpos <= qpos, s, -jnp.inf)
        m_new = jnp.maximum(m, jnp.max(s, axis=-1, keepdims=True))
        alpha = jnp.exp(m - m_new)
        p = jnp.exp(s - m_new)
        l = alpha * l + jnp.sum(p, axis=-1, keepdims=True)
        acc = alpha * acc + jnp.dot(p.astype(BF16), vt, preferred_element_type=F32)
        return m_new, l, acc

    carry = (jnp.full((2 * tq, 1), -jnp.inf, F32), jnp.zeros((2 * tq, 1), F32),
             jnp.zeros((2 * tq, C_HEAD_WIDTH), F32))
    carry = lax.fori_loop(0, i, lambda j, c: step(j, c, False), carry)
    _, l, acc = step(i, carry, True)
    o = acc * (1.0 / l)
    d = o[:tq, :] - lam * o[tq:, :]
    ms = jnp.mean(d * d, axis=-1, keepdims=True)
    y = d * lax.rsqrt(ms + EPS) * sg_ref[...] * (1.0 - lam_init)
    o_ref[...] = (y * cg_ref[...].astype(F32)).astype(BF16)


def _diff(lq, lk, cq, ck, cv, cg, subln_g, lam_init):
    b, t, _ = cq.shape
    tq = min(DIFF_TQ, t)
    assert tq == DIFF_TK
    qmap = lambda bi, h, i: (bi, i, h)
    kvmap = lambda bi, h, i: (bi, 0, h)
    const = lambda bi, h, i: (0, 0)
    return pl.pallas_call(
        functools.partial(_diff_kernel, lam_init=lam_init),
        grid=(b, C_HEADS, t // tq),
        in_specs=[
            pl.BlockSpec((2, HEAD_DIM), const),
            pl.BlockSpec((2, HEAD_DIM), const),
            pl.BlockSpec((None, tq, C_HEAD_WIDTH), qmap),
            pl.BlockSpec((None, t, C_HEAD_WIDTH), kvmap),
            pl.BlockSpec((None, t, C_HEAD_WIDTH), kvmap),
            pl.BlockSpec((None, tq, C_HEAD_WIDTH), qmap),
            pl.BlockSpec((1, C_HEAD_WIDTH), const),
        ],
        out_specs=pl.BlockSpec((None, tq, C_HEAD_WIDTH), qmap),
        out_shape=jax.ShapeDtypeStruct((b, t, C_WIDTH), BF16),
        compiler_params=_params(("parallel", "parallel", "arbitrary")),
        name="diff",
    )(lq, lk, cq, ck, cv, cg, subln_g)


def _merge_kernel(x_ref, ya_ref, yb_ref, yc_ref, mg_ref, wa_ref, wb_ref, wc_ref, wo_ref, o_ref):
    merged = None
    for br, (y_ref, w_ref) in enumerate(((ya_ref, wa_ref), (yb_ref, wb_ref), (yc_ref, wc_ref))):
        proj = jnp.dot(y_ref[...], w_ref[...], preferred_element_type=F32)
        term = mg_ref[:, br * D_MODEL:(br + 1) * D_MODEL].astype(F32) * proj
        merged = term if merged is None else merged + term
    o_ref[...] = x_ref[...] + jnp.dot(merged.astype(BF16), wo_ref[...], preferred_element_type=F32)


def _merge(x2, ya, yb, yc, mg, wa, wb, wc, wo):
    n = x2.shape[0]
    tm = min(ROW_TILE, n)
    row = lambda i: (i, 0)
    const = lambda i: (0, 0)
    return pl.pallas_call(
        _merge_kernel,
        grid=(n // tm,),
        in_specs=[
            pl.BlockSpec((tm, D_MODEL), row),
            pl.BlockSpec((tm, A_WIDTH), row),
            pl.BlockSpec((tm, B_WIDTH), row),
            pl.BlockSpec((tm, C_WIDTH), row),
            pl.BlockSpec((tm, MERGE_COLS), row),
            pl.BlockSpec((A_WIDTH, D_MODEL), const),
            pl.BlockSpec((B_WIDTH, D_MODEL), const),
            pl.BlockSpec((C_WIDTH, D_MODEL), const),
            pl.BlockSpec((D_MODEL, D_MODEL), const),
        ],
        out_specs=pl.BlockSpec((tm, D_MODEL), row),
        out_shape=jax.ShapeDtypeStruct((n, D_MODEL), F32),
        compiler_params=_params(("parallel",)),
        name="merge",
    )(x2, ya, yb, yc, mg, wa, wb, wc, wo)


def kernel(x, norm_g, w_in, attn_q_norm_g, attn_k_norm_g, attn_sinks, w_o_attn, conv_w, conv_b, conv_norm_g, conv_norm_b, w_o_conv, diff_q_norm_g, diff_k_norm_g, lambda_q, lambda_k, diff_subln_g, w_o_diff, w_out):
    b, t, d = x.shape
    n = b * t
    depth = norm_g.shape[0]
    group = lax.broadcasted_iota(jnp.int32, (MXU_COLS, MXU_COLS), 0) // HEAD_DIM
    bd = (group == group.T).astype(BF16)
    tile_gain = lambda g: jnp.tile(g.astype(F32), MXU_COLS // HEAD_DIM).reshape(1, MXU_COLS)
    row = lambda v: v.astype(F32).reshape(1, -1)

    x2 = x.reshape(n, d)
    for l in range(depth):
        aq, ak, av, ag, u, bg, cq, ck, cv, cg, mg = _inproj(
            x2, row(norm_g[l]), w_in[l].astype(BF16), bd,
            tile_gain(attn_q_norm_g[l]), tile_gain(attn_k_norm_g[l]),
            tile_gain(diff_q_norm_g[l]), tile_gain(diff_k_norm_g[l]))
        r3 = lambda a: a.reshape(b, t, a.shape[-1])
        ya = _swa(attn_sinks[l].astype(F32), r3(aq), r3(ak), r3(av), r3(ag))
        yb = _conv(r3(u), conv_w[l].astype(F32), row(conv_b[l]), row(conv_norm_g[l]), row(conv_norm_b[l]), r3(bg))
        lam_init = 0.8 - 0.6 * math.exp(-0.3 * l)
        yc = _diff(lambda_q[l].astype(F32), lambda_k[l].astype(F32), r3(cq), r3(ck), r3(cv), r3(cg),
                   row(diff_subln_g[l]), lam_init)
        x2 = _merge(x2, ya.reshape(n, -1), yb.reshape(n, -1), yc.reshape(n, -1), mg,
                    w_o_attn[l].astype(BF16), w_o_conv[l].astype(BF16), w_o_diff[l].astype(BF16),
                    w_out[l].astype(BF16))
    return x2.reshape(b, t, d)
```

```python
import functools
import math

import jax
import jax.numpy as jnp
from jax import lax
from jax.experimental import pallas as pl
from jax.experimental.pallas import tpu as pltpu

F32 = jnp.float32
BF16 = jnp.bfloat16

D_MODEL = 1024
HEAD_DIM = 64
EPS = 1e-6
A_Q_HEADS = 8
A_KV_HEADS = 2
A_GROUP = A_Q_HEADS // A_KV_HEADS
A_WIDTH = A_Q_HEADS * HEAD_DIM
A_KV_WIDTH = A_KV_HEADS * HEAD_DIM
WINDOW = 128
B_WIDTH = D_MODEL // 2
CONV_WIDTH = 31
C_HEADS = 4
C_HEAD_WIDTH = 2 * HEAD_DIM
C_WIDTH = C_HEADS * C_HEAD_WIDTH
N_BRANCH = 3
MERGE_COLS = N_BRANCH * D_MODEL

OFF_AQ = 0
OFF_AK = OFF_AQ + A_WIDTH
OFF_AV = OFF_AK + A_KV_WIDTH
OFF_AG = OFF_AV + A_KV_WIDTH
OFF_BGLU = OFF_AG + A_WIDTH
OFF_BG = OFF_BGLU + 2 * B_WIDTH
OFF_CQ = OFF_BG + B_WIDTH
OFF_CK = OFF_CQ + C_WIDTH
OFF_CV = OFF_CK + C_WIDTH
OFF_CG = OFF_CV + C_WIDTH
OFF_MG = OFF_CG + C_WIDTH
IN_WIDTH = OFF_MG + MERGE_COLS

MXU_COLS = 256
LANES = 128
VMEM_LIMIT_BYTES = 56 * 1024 * 1024

ROW_TILE = 512
SWA_TILE = 512
CONV_TILE = 256
CONV_HALO = 32
CONV_CHUNK = 64
DIFF_TQ = 256
DIFF_TK = 256
DIFF_HEADS_PER_STEP = 4
DIFF_SUM_ROWS = 16


def _sigmoid(y):
    return 1.0 / (1.0 + jnp.exp(-y))


def _silu(y):
    return y * _sigmoid(y)


def _params(semantics):
    return pltpu.CompilerParams(dimension_semantics=semantics, vmem_limit_bytes=VMEM_LIMIT_BYTES)


def _inproj_kernel(x_ref, g_ref, w_ref, wcvt_ref, bd_ref, aqg_ref, akg_ref, cqg_ref, ckg_ref,
                   aq_o, ak_o, av_o, ag_o, u_o, bg_o, cq_o, ck_o, cvt_o, cg_o, mg_o):
    x = x_ref[...]
    ms = jnp.mean(x * x, axis=-1, keepdims=True)
    h = (x * lax.rsqrt(ms + EPS) * g_ref[...]).astype(BF16)

    def proj(off, width):
        return jnp.dot(h, w_ref[:, off:off + width], preferred_element_type=F32)

    def head_norm(y, gain, scale):
        w = y.shape[-1]
        sq = (y * y).astype(BF16)
        ssum = jnp.dot(sq, bd_ref[:w, :w], preferred_element_type=F32)
        return y * lax.rsqrt(ssum * (1.0 / HEAD_DIM) + EPS) * (gain * scale)

    qscale = HEAD_DIM ** -0.5
    qscale_log2 = qscale * math.log2(math.e)
    for c in range(A_WIDTH // MXU_COLS):
        sl = slice(c * MXU_COLS, (c + 1) * MXU_COLS)
        aq_o[:, sl] = head_norm(proj(OFF_AQ + c * MXU_COLS, MXU_COLS), aqg_ref[...], qscale).astype(BF16)
        ag_o[:, sl] = _silu(proj(OFF_AG + c * MXU_COLS, MXU_COLS)).astype(BF16)
    ak_o[...] = head_norm(proj(OFF_AK, A_KV_WIDTH), akg_ref[:, :A_KV_WIDTH], 1.0).astype(BF16)
    av_o[...] = proj(OFF_AV, A_KV_WIDTH).astype(BF16)
    for c in range(B_WIDTH // MXU_COLS):
        sl = slice(c * MXU_COLS, (c + 1) * MXU_COLS)
        a = proj(OFF_BGLU + c * MXU_COLS, MXU_COLS)
        b = proj(OFF_BGLU + B_WIDTH + c * MXU_COLS, MXU_COLS)
        u_o[:, sl] = (a * _sigmoid(b)).astype(BF16)
        bg_o[:, sl] = _silu(proj(OFF_BG + c * MXU_COLS, MXU_COLS)).astype(BF16)
    for c in range(C_WIDTH // MXU_COLS):
        sl = slice(c * MXU_COLS, (c + 1) * MXU_COLS)
        cq_o[:, sl] = head_norm(proj(OFF_CQ + c * MXU_COLS, MXU_COLS), cqg_ref[...], qscale_log2).astype(BF16)
        ck_o[:, sl] = head_norm(proj(OFF_CK + c * MXU_COLS, MXU_COLS), ckg_ref[...], 1.0).astype(BF16)
        cg_o[:, sl] = _silu(proj(OFF_CG + c * MXU_COLS, MXU_COLS)).astype(BF16)
    cvt_o[...] = lax.dot_general(wcvt_ref[...], h, (((1,), (1,)), ((), ())),
                                 preferred_element_type=F32).astype(BF16)
    for c in range(MERGE_COLS // MXU_COLS):
        sl = slice(c * MXU_COLS, (c + 1) * MXU_COLS)
        mg_o[:, sl] = _sigmoid(proj(OFF_MG + c * MXU_COLS, MXU_COLS)).astype(BF16)


def _inproj(x2, batch, norm_g, w_in, w_cvt, bd, aqg, akg, cqg, ckg):
    n = x2.shape[0]
    t = n // batch
    tm = min(ROW_TILE, t)
    per = t // tm
    widths = (A_WIDTH, A_KV_WIDTH, A_KV_WIDTH, A_WIDTH, B_WIDTH, B_WIDTH,
              C_WIDTH, C_WIDTH, None, C_WIDTH, MERGE_COLS)
    const = lambda i: (0, 0)
    row = lambda i: (i, 0)
    cvt_spec = pl.BlockSpec((None, C_WIDTH, tm), lambda i: (i // per, 0, i % per))
    cvt_shape = jax.ShapeDtypeStruct((batch, C_WIDTH, t), BF16)
    return pl.pallas_call(
        _inproj_kernel,
        grid=(n // tm,),
        in_specs=[
            pl.BlockSpec((tm, D_MODEL), row),
            pl.BlockSpec((1, D_MODEL), const),
            pl.BlockSpec((D_MODEL, IN_WIDTH), const, pipeline_mode=pl.Buffered(1)),
            pl.BlockSpec((C_WIDTH, D_MODEL), const, pipeline_mode=pl.Buffered(1)),
            pl.BlockSpec((MXU_COLS, MXU_COLS), const),
            pl.BlockSpec((1, MXU_COLS), const),
            pl.BlockSpec((1, MXU_COLS), const),
            pl.BlockSpec((1, MXU_COLS), const),
            pl.BlockSpec((1, MXU_COLS), const),
        ],
        out_specs=[cvt_spec if w is None else pl.BlockSpec((tm, w), row) for w in widths],
        out_shape=[cvt_shape if w is None else jax.ShapeDtypeStruct((n, w), BF16) for w in widths],
        compiler_params=_params(("parallel",)),
        name="inproj",
    )(x2, norm_g, w_in, w_cvt, bd, aqg, akg, cqg, ckg)


def _swa_kernel(sink_ref, q_ref, kc_ref, kp_ref, vc_ref, vp_ref, ag_ref, o_ref):
    i = pl.program_id(1)
    nblk = q_ref.shape[0] // WINDOW
    kfull = jnp.concatenate([kp_ref[...], kc_ref[...]], axis=0)
    vfull = jnp.concatenate([vp_ref[...], vc_ref[...]], axis=0)
    rows = A_GROUP * WINDOW
    t = lax.broadcasted_iota(jnp.int32, (rows, 2 * WINDOW), 0) & (WINDOW - 1)
    c = lax.broadcasted_iota(jnp.int32, (rows, 2 * WINDOW), 1)
    local = (c > t) & (c <= t + WINDOW)
    head_row = lax.broadcasted_iota(jnp.int32, (rows, 1), 0) >> int(math.log2(WINDOW))
    for j in range(nblk):
        q = q_ref[j * WINDOW:(j + 1) * WINDOW, :]
        kwin = kfull[j * WINDOW:(j + 2) * WINDOW, :]
        vwin = vfull[j * WINDOW:(j + 2) * WINDOW, :]
        if j == 0:
            mask = local & ((c >= WINDOW) | (i > 0))
        else:
            mask = local
        heads_out = []
        for kv in range(A_KV_HEADS):
            kh = kwin[:, kv * HEAD_DIM:(kv + 1) * HEAD_DIM]
            vh = vwin[:, kv * HEAD_DIM:(kv + 1) * HEAD_DIM]
            qs = jnp.concatenate(
                [q[:, (kv * A_GROUP + g) * HEAD_DIM:(kv * A_GROUP + g + 1) * HEAD_DIM] for g in range(A_GROUP)],
                axis=0)
            sink = jnp.zeros((rows, 1), F32)
            for g in range(A_GROUP):
                sink = jnp.where(head_row == g, sink_ref[kv * A_GROUP + g], sink)
            s = lax.dot_general(qs, kh, (((1,), (1,)), ((), ())), preferred_element_type=F32)
            s = jnp.where(mask, s, -jnp.inf)
            m = jnp.maximum(jnp.max(s, axis=-1, keepdims=True), sink)
            p = jnp.exp(s - m)
            denom = jnp.sum(p, axis=-1, keepdims=True) + jnp.exp(sink - m)
            p = p * (1.0 / denom)
            o = jnp.dot(p.astype(BF16), vh, preferred_element_type=F32)
            heads_out.extend(o[g * WINDOW:(g + 1) * WINDOW, :] for g in range(A_GROUP))
        o_all = jnp.concatenate(heads_out, axis=1)
        gate = ag_ref[j * WINDOW:(j + 1) * WINDOW, :].astype(F32)
        o_ref[j * WINDOW:(j + 1) * WINDOW, :] = (o_all * gate).astype(BF16)


def _swa(sinks, aq, ak, av, ag):
    b, t, _ = aq.shape
    tq = min(SWA_TILE, t)
    per = tq // WINDOW
    cur = lambda bi, i: (bi, i, 0)
    prev = lambda bi, i: (bi, jnp.maximum(i * per - 1, 0), 0)
    return pl.pallas_call(
        _swa_kernel,
        grid=(b, t // tq),
        in_specs=[
            pl.BlockSpec(memory_space=pltpu.SMEM),
            pl.BlockSpec((None, tq, A_WIDTH), cur),
            pl.BlockSpec((None, tq, A_KV_WIDTH), cur),
            pl.BlockSpec((None, WINDOW, A_KV_WIDTH), prev),
            pl.BlockSpec((None, tq, A_KV_WIDTH), cur),
            pl.BlockSpec((None, WINDOW, A_KV_WIDTH), prev),
            pl.BlockSpec((None, tq, A_WIDTH), cur),
        ],
        out_specs=pl.BlockSpec((None, tq, A_WIDTH), cur),
        out_shape=jax.ShapeDtypeStruct((b, t, A_WIDTH), BF16),
        compiler_params=_params(("parallel", "parallel")),
        name="swa",
    )(sinks, aq, ak, ak, av, av, ag)


def _conv_kernel(uc_ref, up_ref, w_ref, cb_ref, lg_ref, lb_ref, bg_ref, o_ref, ubuf):
    i = pl.program_id(1)
    tt = uc_ref.shape[0]
    halo = up_ref[...].astype(F32)
    ubuf[0:CONV_HALO, :] = jnp.where(i > 0, halo, 0.0)
    ubuf[CONV_HALO:, :] = uc_ref[...].astype(F32)
    first = CONV_HALO - (CONV_WIDTH - 1)
    for r0 in range(0, tt, CONV_CHUNK):
        acc = jnp.zeros((CONV_CHUNK, B_WIDTH), F32)
        for k in range(CONV_WIDTH):
            acc = acc + w_ref[k:k + 1, :] * ubuf[r0 + first + k:r0 + first + k + CONV_CHUNK, :]
        y = acc + cb_ref[...]
        mu = jnp.mean(y, axis=-1, keepdims=True)
        yc = y - mu
        var = jnp.mean(yc * yc, axis=-1, keepdims=True)
        yn = yc * lax.rsqrt(var + EPS) * lg_ref[...] + lb_ref[...]
        gate = bg_ref[r0:r0 + CONV_CHUNK, :].astype(F32)
        o_ref[r0:r0 + CONV_CHUNK, :] = (_silu(yn) * gate).astype(BF16)


def _conv(u, conv_w, conv_b, ln_g, ln_b, bg):
    b, t, _ = u.shape
    tt = min(CONV_TILE, t)
    per = tt // CONV_HALO
    cur = lambda bi, i: (bi, i, 0)
    prev = lambda bi, i: (bi, jnp.maximum(i * per - 1, 0), 0)
    const = lambda bi, i: (0, 0)
    return pl.pallas_call(
        _conv_kernel,
        grid=(b, t // tt),
        in_specs=[
            pl.BlockSpec((None, tt, B_WIDTH), cur),
            pl.BlockSpec((None, CONV_HALO, B_WIDTH), prev),
            pl.BlockSpec((CONV_WIDTH, B_WIDTH), const),
            pl.BlockSpec((1, B_WIDTH), const),
            pl.BlockSpec((1, B_WIDTH), const),
            pl.BlockSpec((1, B_WIDTH), const),
            pl.BlockSpec((None, tt, B_WIDTH), cur),
        ],
        out_specs=pl.BlockSpec((None, tt, B_WIDTH), cur),
        out_shape=jax.ShapeDtypeStruct((b, t, B_WIDTH), BF16),
        scratch_shapes=[pltpu.VMEM((tt + CONV_HALO, B_WIDTH), F32)],
        compiler_params=_params(("parallel", "parallel")),
        name="conv",
    )(u, u, conv_w, conv_b, ln_g, ln_b, bg)


def _diff_kernel(lq_ref, lk_ref, q_ref, k_ref, vt_ref, cg_ref, sg_ref, o_ref, *, lam_init):
    i = pl.program_id(2)
    tq = q_ref.shape[0]
    tk = DIFF_TK
    heads = q_ref.shape[1] // C_HEAD_WIDTH
    e = jnp.exp(jnp.sum(lq_ref[...] * lk_ref[...], axis=-1, keepdims=True))
    lam = e[0:1, :] - e[1:2, :] + lam_init

    lane = lax.broadcasted_iota(jnp.int32, (tq, C_HEAD_WIDTH), 1)
    qs = []
    for h in range(heads):
        q = q_ref[:, h * C_HEAD_WIDTH:(h + 1) * C_HEAD_WIDTH]
        zero = jnp.zeros_like(q)
        qs.append(jnp.concatenate([jnp.where(lane < HEAD_DIM, q, zero), jnp.where(lane >= HEAD_DIM, q, zero)],
                                  axis=0))

    def scores(h, start):
        kt = k_ref[pl.ds(start, tk), h * C_HEAD_WIDTH:(h + 1) * C_HEAD_WIDTH]
        return lax.dot_general(kt, qs[h], (((1,), (1,)), ((), ())), preferred_element_type=F32)

    ones_rows = jnp.ones((DIFF_SUM_ROWS, tk), BF16)

    def head_step(h, start, s, carry, masked):
        m, acc = carry
        vt = vt_ref[h * C_HEAD_WIDTH:(h + 1) * C_HEAD_WIDTH, pl.ds(start, tk)]
        vt = jnp.concatenate([vt, ones_rows], axis=0)
        if masked:
            kpos = lax.broadcasted_iota(jnp.int32, s.shape, 0)
            qpos = lax.broadcasted_iota(jnp.int32, s.shape, 1) & (tq - 1)
            s = jnp.where(kpos <= qpos, s, -jnp.inf)
        m_new = jnp.maximum(m, jnp.max(s, axis=0, keepdims=True))
        alpha = jnp.exp2(m - m_new)
        p = jnp.exp2(s - m_new)
        acc = alpha * acc + jnp.dot(vt, p.astype(BF16), preferred_element_type=F32)
        return m_new, acc

    def step(j, carries, masked):
        start = pl.multiple_of(j * tk, tk)
        s_all = [scores(h, start) for h in range(heads)]
        return tuple(head_step(h, start, s_all[h], carries[h], masked) for h in range(heads))

    init = (jnp.full((1, 2 * tq), -jnp.inf, F32), jnp.zeros((C_HEAD_WIDTH + DIFF_SUM_ROWS, 2 * tq), F32))
    carries = lax.fori_loop(0, i, lambda j, c: step(j, c, False), (init,) * heads)
    carries = step(i, carries, True)
    for h in range(heads):
        _, acc = carries[h]
        o = acc[:C_HEAD_WIDTH, :] * (1.0 / acc[C_HEAD_WIDTH:C_HEAD_WIDTH + 1, :])
        d = o[:, :tq] - lam * o[:, tq:]
        ms = jnp.mean(d * d, axis=0, keepdims=True)
        y = (d * lax.rsqrt(ms + EPS)).T
        y = y * (sg_ref[...] * (1.0 - lam_init))
        sl = slice(h * C_HEAD_WIDTH, (h + 1) * C_HEAD_WIDTH)
        o_ref[:, sl] = (y * cg_ref[:, sl].astype(F32)).astype(BF16)


def _diff(lq, lk, cq, ck, cvt, cg, subln_g, lam_init):
    b, t, _ = cq.shape
    tq = min(DIFF_TQ, t)
    assert tq == DIFF_TK
    width = DIFF_HEADS_PER_STEP * C_HEAD_WIDTH
    qmap = lambda bi, h, i: (bi, i, h)
    kmap = lambda bi, h, i: (bi, 0, h)
    vtmap = lambda bi, h, i: (bi, h, 0)
    const = lambda bi, h, i: (0, 0)
    return pl.pallas_call(
        functools.partial(_diff_kernel, lam_init=lam_init),
        grid=(b, C_HEADS // DIFF_HEADS_PER_STEP, t // tq),
        in_specs=[
            pl.BlockSpec((2, HEAD_DIM), const),
            pl.BlockSpec((2, HEAD_DIM), const),
            pl.BlockSpec((None, tq, width), qmap),
            pl.BlockSpec((None, t, width), kmap),
            pl.BlockSpec((None, width, t), vtmap),
            pl.BlockSpec((None, tq, width), qmap),
            pl.BlockSpec((1, C_HEAD_WIDTH), const),
        ],
        out_specs=pl.BlockSpec((None, tq, width), qmap),
        out_shape=jax.ShapeDtypeStruct((b, t, C_WIDTH), BF16),
        compiler_params=_params(("parallel", "parallel", "arbitrary")),
        name="diff",
    )(lq, lk, cq, ck, cvt, cg, subln_g)


def _merge_kernel(x_ref, ya_ref, yb_ref, yc_ref, mg_ref, wa_ref, wb_ref, wc_ref, wo_ref, o_ref):
    merged = None
    for br, (y_ref, w_ref) in enumerate(((ya_ref, wa_ref), (yb_ref, wb_ref), (yc_ref, wc_ref))):
        proj = jnp.dot(y_ref[...], w_ref[...], preferred_element_type=F32)
        term = mg_ref[:, br * D_MODEL:(br + 1) * D_MODEL].astype(F32) * proj
        merged = term if merged is None else merged + term
    o_ref[...] = x_ref[...] + jnp.dot(merged.astype(BF16), wo_ref[...], preferred_element_type=F32)


def _merge(x2, ya, yb, yc, mg, wa, wb, wc, wo):
    n = x2.shape[0]
    tm = min(ROW_TILE, n)
    row = lambda i: (i, 0)
    const = lambda i: (0, 0)
    return pl.pallas_call(
        _merge_kernel,
        grid=(n // tm,),
        in_specs=[
            pl.BlockSpec((tm, D_MODEL), row),
            pl.BlockSpec((tm, A_WIDTH), row),
            pl.BlockSpec((tm, B_WIDTH), row),
            pl.BlockSpec((tm, C_WIDTH), row),
            pl.BlockSpec((tm, MERGE_COLS), row),
            pl.BlockSpec((A_WIDTH, D_MODEL), const),
            pl.BlockSpec((B_WIDTH, D_MODEL), const),
            pl.BlockSpec((C_WIDTH, D_MODEL), const),
            pl.BlockSpec((D_MODEL, D_MODEL), const),
        ],
        out_specs=pl.BlockSpec((tm, D_MODEL), row),
        out_shape=jax.ShapeDtypeStruct((n, D_MODEL), F32),
        compiler_params=_params(("parallel",)),
        name="merge",
    )(x2, ya, yb, yc, mg, wa, wb, wc, wo)


def kernel(x, norm_g, w_in, attn_q_norm_g, attn_k_norm_g, attn_sinks, w_o_attn, conv_w, conv_b, conv_norm_g, conv_norm_b, w_o_conv, diff_q_norm_g, diff_k_norm_g, lambda_q, lambda_k, diff_subln_g, w_o_diff, w_out):
    b, t, d = x.shape
    n = b * t
    depth = norm_g.shape[0]
    group = lax.broadcasted_iota(jnp.int32, (MXU_COLS, MXU_COLS), 0) // HEAD_DIM
    bd = (group == group.T).astype(BF16)
    tile_gain = lambda g: jnp.tile(g.astype(F32), MXU_COLS // HEAD_DIM).reshape(1, MXU_COLS)
    row = lambda v: v.astype(F32).reshape(1, -1)

    x2 = x.reshape(n, d)
    for l in range(depth):
        w_l = w_in[l].astype(BF16)
        aq, ak, av, ag, u, bg, cq, ck, cvt, cg, mg = _inproj(
            x2, b, row(norm_g[l]), w_l, w_l[:, OFF_CV:OFF_CV + C_WIDTH].T, bd,
            tile_gain(attn_q_norm_g[l]), tile_gain(attn_k_norm_g[l]),
            tile_gain(diff_q_norm_g[l]), tile_gain(diff_k_norm_g[l]))
        r3 = lambda a: a.reshape(b, t, a.shape[-1])
        ya = _swa(attn_sinks[l].astype(F32), r3(aq), r3(ak), r3(av), r3(ag))
        yb = _conv(r3(u), conv_w[l].astype(F32), row(conv_b[l]), row(conv_norm_g[l]), row(conv_norm_b[l]), r3(bg))
        lam_init = 0.8 - 0.6 * math.exp(-0.3 * l)
        yc = _diff(lambda_q[l].astype(F32), lambda_k[l].astype(F32), r3(cq), r3(ck), cvt, r3(cg),
                   row(diff_subln_g[l]), lam_init)
        x2 = _merge(x2, ya.reshape(n, -1), yb.reshape(n, -1), yc.reshape(n, -1), mg,
                    w_o_attn[l].astype(BF16), w_o_conv[l].astype(BF16), w_o_diff[l].astype(BF16),
                    w_out[l].astype(BF16))
    return x2.reshape(b, t, d)
```

```python
import functools
import math

import jax
import jax.numpy as jnp
from jax import lax
from jax.experimental import pallas as pl
from jax.experimental.pallas import tpu as pltpu

F32 = jnp.float32
BF16 = jnp.bfloat16

D_MODEL = 1024
HEAD_DIM = 64
EPS = 1e-6
A_Q_HEADS = 8
A_KV_HEADS = 2
A_GROUP = A_Q_HEADS // A_KV_HEADS
A_WIDTH = A_Q_HEADS * HEAD_DIM
A_KV_WIDTH = A_KV_HEADS * HEAD_DIM
WINDOW = 128
B_WIDTH = D_MODEL // 2
CONV_WIDTH = 31
C_HEADS = 4
C_HEAD_WIDTH = 2 * HEAD_DIM
C_WIDTH = C_HEADS * C_HEAD_WIDTH
N_BRANCH = 3
MERGE_COLS = N_BRANCH * D_MODEL

OFF_AQ = 0
OFF_AK = OFF_AQ + A_WIDTH
OFF_AV = OFF_AK + A_KV_WIDTH
OFF_AG = OFF_AV + A_KV_WIDTH
OFF_BGLU = OFF_AG + A_WIDTH
OFF_BG = OFF_BGLU + 2 * B_WIDTH
OFF_CQ = OFF_BG + B_WIDTH
OFF_CK = OFF_CQ + C_WIDTH
OFF_CV = OFF_CK + C_WIDTH
OFF_CG = OFF_CV + C_WIDTH
OFF_MG = OFF_CG + C_WIDTH
IN_WIDTH = OFF_MG + MERGE_COLS

MXU_COLS = 256
LANES = 128
SUBLANES = 8
VMEM_LIMIT_BYTES = 56 * 1024 * 1024

ROW_TILE = 512
SWA_TILE = 512
CONV_TILE = 256
CONV_HALO = 32
CONV_CHUNK = 64
DIFF_TQ = 512
DIFF_TK = 512
DIFF_HEADS_PER_STEP = 2
DIFF_SUM_ROWS = 16


def _sigmoid(y):
    return 1.0 / (1.0 + jnp.exp(-y))


def _silu(y):
    return y * _sigmoid(y)


def _params(semantics):
    return pltpu.CompilerParams(dimension_semantics=semantics, vmem_limit_bytes=VMEM_LIMIT_BYTES)


def _inproj_kernel(x_ref, g_ref, w_ref, wak2_ref, wavt_ref, wcvt_ref, bd_ref, aqg_ref, akg_ref, cqg_ref, ckg_ref,
                   aq_o, ak2_o, avt_o, ag_o, u_o, bg_o, cq_o, ck_o, cvt_o, cg_o, mg_o):
    x = x_ref[...]
    ms = jnp.mean(x * x, axis=-1, keepdims=True)
    h = (x * lax.rsqrt(ms + EPS) * g_ref[...]).astype(BF16)

    def proj(off, width):
        return jnp.dot(h, w_ref[:, off:off + width], preferred_element_type=F32)

    def head_norm(y, gain, scale):
        w = y.shape[-1]
        sq = (y * y).astype(BF16)
        ssum = jnp.dot(sq, bd_ref[:w, :w], preferred_element_type=F32)
        return y * lax.rsqrt(ssum * (1.0 / HEAD_DIM) + EPS) * (gain * scale)

    qscale_log2 = HEAD_DIM ** -0.5 * math.log2(math.e)
    nt = (((1,), (1,)), ((), ()))
    for c in range(A_WIDTH // MXU_COLS):
        sl = slice(c * MXU_COLS, (c + 1) * MXU_COLS)
        aq_o[:, sl] = head_norm(proj(OFF_AQ + c * MXU_COLS, MXU_COLS), aqg_ref[...], qscale_log2).astype(BF16)
        ag_o[:, sl] = _silu(proj(OFF_AG + c * MXU_COLS, MXU_COLS)).astype(BF16)
    ak2 = jnp.dot(h, wak2_ref[...], preferred_element_type=F32)
    ak2_o[...] = head_norm(ak2, akg_ref[...], 1.0).astype(BF16)
    avt_o[...] = lax.dot_general(wavt_ref[...], h, nt, preferred_element_type=F32).astype(BF16)
    for c in range(B_WIDTH // MXU_COLS):
        sl = slice(c * MXU_COLS, (c + 1) * MXU_COLS)
        a = proj(OFF_BGLU + c * MXU_COLS, MXU_COLS)
        b = proj(OFF_BGLU + B_WIDTH + c * MXU_COLS, MXU_COLS)
        u_o[:, sl] = (a * _sigmoid(b)).astype(BF16)
        bg_o[:, sl] = _silu(proj(OFF_BG + c * MXU_COLS, MXU_COLS)).astype(BF16)
    for c in range(C_WIDTH // MXU_COLS):
        sl = slice(c * MXU_COLS, (c + 1) * MXU_COLS)
        cq_o[:, sl] = head_norm(proj(OFF_CQ + c * MXU_COLS, MXU_COLS), cqg_ref[...], qscale_log2).astype(BF16)
        ck_o[:, sl] = head_norm(proj(OFF_CK + c * MXU_COLS, MXU_COLS), ckg_ref[...], 1.0).astype(BF16)
        cg_o[:, sl] = _silu(proj(OFF_CG + c * MXU_COLS, MXU_COLS)).astype(BF16)
    cvt_o[...] = lax.dot_general(wcvt_ref[...], h, nt, preferred_element_type=F32).astype(BF16)
    for c in range(MERGE_COLS // MXU_COLS):
        sl = slice(c * MXU_COLS, (c + 1) * MXU_COLS)
        mg_o[:, sl] = _sigmoid(proj(OFF_MG + c * MXU_COLS, MXU_COLS)).astype(BF16)


def _inproj(x2, batch, norm_g, w_in, w_ak2, w_avt, w_cvt, bd, aqg, akg, cqg, ckg):
    n = x2.shape[0]
    t = n // batch
    tm = min(ROW_TILE, t)
    per = t // tm
    const = lambda i: (0, 0)
    row = lambda i: (i, 0)
    rows = lambda w: (pl.BlockSpec((tm, w), row), jax.ShapeDtypeStruct((n, w), BF16))
    cols = lambda w: (pl.BlockSpec((None, w, tm), lambda i: (i // per, 0, i % per)),
                      jax.ShapeDtypeStruct((batch, w, t), BF16))
    outs = (rows(A_WIDTH), rows(2 * A_KV_WIDTH), cols(A_KV_WIDTH), rows(A_WIDTH), rows(B_WIDTH), rows(B_WIDTH),
            rows(C_WIDTH), rows(C_WIDTH), cols(C_WIDTH), rows(C_WIDTH), rows(MERGE_COLS))
    resident = lambda shape: pl.BlockSpec(shape, const, pipeline_mode=pl.Buffered(1))
    return pl.pallas_call(
        _inproj_kernel,
        grid=(n // tm,),
        in_specs=[
            pl.BlockSpec((tm, D_MODEL), row),
            pl.BlockSpec((1, D_MODEL), const),
            resident((D_MODEL, IN_WIDTH)),
            resident((D_MODEL, 2 * A_KV_WIDTH)),
            resident((A_KV_WIDTH, D_MODEL)),
            resident((C_WIDTH, D_MODEL)),
            pl.BlockSpec((MXU_COLS, MXU_COLS), const),
            pl.BlockSpec((1, MXU_COLS), const),
            pl.BlockSpec((1, MXU_COLS), const),
            pl.BlockSpec((1, MXU_COLS), const),
            pl.BlockSpec((1, MXU_COLS), const),
        ],
        out_specs=[o[0] for o in outs],
        out_shape=[o[1] for o in outs],
        compiler_params=_params(("parallel",)),
        name="inproj",
    )(x2, norm_g, w_in, w_ak2, w_avt, w_cvt, bd, aqg, akg, cqg, ckg)


def _swa_kernel(sink_ref, q_ref, kc_ref, kp_ref, vtc_ref, vtp_ref, ag_ref, o_ref):
    i = pl.program_id(1)
    nblk = q_ref.shape[0] // WINDOW
    kfull = jnp.concatenate([kp_ref[...], kc_ref[...]], axis=0)
    vtfull = jnp.concatenate([vtp_ref[...], vtc_ref[...]], axis=1)
    pair = 2 * WINDOW
    c = lax.broadcasted_iota(jnp.int32, (pair, pair), 0)
    t = lax.broadcasted_iota(jnp.int32, (pair, pair), 1) & (WINDOW - 1)
    local = (c > t) & (c <= t + WINDOW)
    first_col = lax.broadcasted_iota(jnp.int32, (1, pair), 1) < WINDOW
    half = lax.broadcasted_iota(jnp.int32, (WINDOW, LANES), 1) < HEAD_DIM
    ones_rows = jnp.ones((DIFF_SUM_ROWS, pair), BF16)
    log2e = math.log2(math.e)
    chains = [(j, kv, par) for j in range(nblk) for kv in range(A_KV_HEADS) for par in range(2)]

    def scores(j, kv, par):
        q = q_ref[j * WINDOW:(j + 1) * WINDOW, :]
        kwin = kfull[j * WINDOW:(j + 2) * WINDOW, :]
        qz = []
        for h in (kv * A_GROUP + par, kv * A_GROUP + par + 2):
            blk = q[:, (h // 2) * LANES:(h // 2 + 1) * LANES]
            keep = half if par == 0 else jnp.logical_not(half)
            qz.append(jnp.where(keep, blk, jnp.zeros_like(blk)))
        qz = jnp.concatenate(qz, axis=0)
        ksel = kwin[:, :LANES] if par == kv else kwin[:, LANES:]
        return lax.dot_general(ksel, qz, (((1,), (1,)), ((), ())), preferred_element_type=F32)

    s_all = [scores(*ch) for ch in chains]
    pieces = {}
    for (j, kv, par), s in zip(chains, s_all):
        heads = (kv * A_GROUP + par, kv * A_GROUP + par + 2)
        mask = local & ((c >= WINDOW) | (i > 0)) if j == 0 else local
        s = jnp.where(mask, s, -jnp.inf)
        sink = jnp.where(first_col, sink_ref[heads[0]] * log2e, sink_ref[heads[1]] * log2e)
        m = jnp.maximum(jnp.max(s, axis=0, keepdims=True), sink)
        p = jnp.exp2(s - m).astype(BF16)
        vtwin = vtfull[kv * HEAD_DIM:(kv + 1) * HEAD_DIM, j * WINDOW:(j + 2) * WINDOW]
        acc = jnp.dot(jnp.concatenate([vtwin, ones_rows], axis=0), p, preferred_element_type=F32)
        denom = acc[HEAD_DIM:HEAD_DIM + 1, :] + jnp.exp2(sink - m)
        o = acc[:HEAD_DIM, :] * (1.0 / denom)
        pieces[j, heads[0]] = o[:, :WINDOW]
        pieces[j, heads[1]] = o[:, WINDOW:]
    for j in range(nblk):
        cols = [jnp.concatenate([pieces[j, 2 * cb], pieces[j, 2 * cb + 1]], axis=0).T for cb in range(A_Q_HEADS // 2)]
        o_all = jnp.concatenate(cols, axis=1)
        gate = ag_ref[j * WINDOW:(j + 1) * WINDOW, :].astype(F32)
        o_ref[j * WINDOW:(j + 1) * WINDOW, :] = (o_all * gate).astype(BF16)


def _swa(sinks, aq, ak2, avt, ag):
    b, t, _ = aq.shape
    tq = min(SWA_TILE, t)
    per = tq // WINDOW
    cur = lambda bi, i: (bi, i, 0)
    prev = lambda bi, i: (bi, jnp.maximum(i * per - 1, 0), 0)
    cur_t = lambda bi, i: (bi, 0, i)
    prev_t = lambda bi, i: (bi, 0, jnp.maximum(i * per - 1, 0))
    return pl.pallas_call(
        _swa_kernel,
        grid=(b, t // tq),
        in_specs=[
            pl.BlockSpec(memory_space=pltpu.SMEM),
            pl.BlockSpec((None, tq, A_WIDTH), cur),
            pl.BlockSpec((None, tq, 2 * A_KV_WIDTH), cur),
            pl.BlockSpec((None, WINDOW, 2 * A_KV_WIDTH), prev),
            pl.BlockSpec((None, A_KV_WIDTH, tq), cur_t),
            pl.BlockSpec((None, A_KV_WIDTH, WINDOW), prev_t),
            pl.BlockSpec((None, tq, A_WIDTH), cur),
        ],
        out_specs=pl.BlockSpec((None, tq, A_WIDTH), cur),
        out_shape=jax.ShapeDtypeStruct((b, t, A_WIDTH), BF16),
        compiler_params=_params(("parallel", "parallel")),
        name="swa",
    )(sinks, aq, ak2, ak2, avt, avt, ag)


def _conv_kernel(uc_ref, up_ref, w_ref, cb_ref, lg_ref, lb_ref, bg_ref, o_ref, ubuf):
    i = pl.program_id(1)
    tt = uc_ref.shape[0]
    halo = up_ref[...].astype(F32)
    ubuf[0, 0:CONV_HALO, :] = jnp.where(i > 0, halo, 0.0)
    ubuf[0, CONV_HALO:, :] = uc_ref[...].astype(F32)
    shifted_rows = tt + CONV_HALO - SUBLANES
    for s in range(1, SUBLANES):
        ubuf[s, 0:shifted_rows, :] = ubuf[0, s:s + shifted_rows, :]
    first = CONV_HALO - (CONV_WIDTH - 1)
    for r0 in range(0, tt, CONV_CHUNK):
        acc = jnp.zeros((CONV_CHUNK, B_WIDTH), F32)
        for k in range(CONV_WIDTH):
            shift = (first + k) % SUBLANES
            base = r0 + first + k - shift
            acc = acc + w_ref[k:k + 1, :] * ubuf[shift, base:base + CONV_CHUNK, :]
        y = acc + cb_ref[...]
        mu = jnp.mean(y, axis=-1, keepdims=True)
        yc = y - mu
        var = jnp.mean(yc * yc, axis=-1, keepdims=True)
        yn = yc * lax.rsqrt(var + EPS) * lg_ref[...] + lb_ref[...]
        gate = bg_ref[r0:r0 + CONV_CHUNK, :].astype(F32)
        o_ref[r0:r0 + CONV_CHUNK, :] = (_silu(yn) * gate).astype(BF16)


def _conv(u, conv_w, conv_b, ln_g, ln_b, bg):
    b, t, _ = u.shape
    tt = min(CONV_TILE, t)
    per = tt // CONV_HALO
    cur = lambda bi, i: (bi, i, 0)
    prev = lambda bi, i: (bi, jnp.maximum(i * per - 1, 0), 0)
    const = lambda bi, i: (0, 0)
    return pl.pallas_call(
        _conv_kernel,
        grid=(b, t // tt),
        in_specs=[
            pl.BlockSpec((None, tt, B_WIDTH), cur),
            pl.BlockSpec((None, CONV_HALO, B_WIDTH), prev),
            pl.BlockSpec((CONV_WIDTH, B_WIDTH), const),
            pl.BlockSpec((1, B_WIDTH), const),
            pl.BlockSpec((1, B_WIDTH), const),
            pl.BlockSpec((1, B_WIDTH), const),
            pl.BlockSpec((None, tt, B_WIDTH), cur),
        ],
        out_specs=pl.BlockSpec((None, tt, B_WIDTH), cur),
        out_shape=jax.ShapeDtypeStruct((b, t, B_WIDTH), BF16),
        scratch_shapes=[pltpu.VMEM((SUBLANES, tt + CONV_HALO, B_WIDTH), F32)],
        compiler_params=_params(("parallel", "parallel")),
        name="conv",
    )(u, u, conv_w, conv_b, ln_g, ln_b, bg)


def _diff_kernel(lq_ref, lk_ref, q_ref, k_ref, vt_ref, cg_ref, sg_ref, o_ref, *, lam_init):
    i = pl.program_id(2)
    tq = q_ref.shape[0]
    tk = DIFF_TK
    heads = q_ref.shape[1] // C_HEAD_WIDTH
    e = jnp.exp(jnp.sum(lq_ref[...] * lk_ref[...], axis=-1, keepdims=True))
    lam = e[0:1, :] - e[1:2, :] + lam_init

    lane = lax.broadcasted_iota(jnp.int32, (tq, C_HEAD_WIDTH), 1)
    qs = []
    for h in range(heads):
        q = q_ref[:, h * C_HEAD_WIDTH:(h + 1) * C_HEAD_WIDTH]
        zero = jnp.zeros_like(q)
        qs.append(jnp.concatenate([jnp.where(lane < HEAD_DIM, q, zero), jnp.where(lane >= HEAD_DIM, q, zero)],
                                  axis=0))

    def scores(h, start):
        kt = k_ref[pl.ds(start, tk), h * C_HEAD_WIDTH:(h + 1) * C_HEAD_WIDTH]
        return lax.dot_general(kt, qs[h], (((1,), (1,)), ((), ())), preferred_element_type=F32)

    ones_rows = jnp.ones((DIFF_SUM_ROWS, tk), BF16)

    def head_step(h, start, s, carry, masked):
        m, acc = carry
        vt = vt_ref[h * C_HEAD_WIDTH:(h + 1) * C_HEAD_WIDTH, pl.ds(start, tk)]
        vt = jnp.concatenate([vt, ones_rows], axis=0)
        if masked:
            kpos = lax.broadcasted_iota(jnp.int32, s.shape, 0)
            qpos = lax.broadcasted_iota(jnp.int32, s.shape, 1) & (tq - 1)
            s = jnp.where(kpos <= qpos, s, -jnp.inf)
        m_blocks, p_blocks = [], []
        for c in range(0, s.shape[1], LANES):
            sc = s[:, c:c + LANES]
            mc = jnp.maximum(m[:, c:c + LANES], jnp.max(sc, axis=0, keepdims=True))
            m_blocks.append(mc)
            p_blocks.append(jnp.exp2(sc - mc).astype(BF16))
        m_new = jnp.concatenate(m_blocks, axis=1)
        p = jnp.concatenate(p_blocks, axis=1)
        alpha = jnp.exp2(m - m_new)
        acc = alpha * acc + jnp.dot(vt, p, preferred_element_type=F32)
        return m_new, acc

    def step(j, carries, masked):
        start = pl.multiple_of(j * tk, tk)
        s_all = [scores(h, start) for h in range(heads)]
        return tuple(head_step(h, start, s_all[h], carries[h], masked) for h in range(heads))

    init = (jnp.full((1, 2 * tq), -jnp.inf, F32), jnp.zeros((C_HEAD_WIDTH + DIFF_SUM_ROWS, 2 * tq), F32))
    carries = lax.fori_loop(0, i, lambda j, c: step(j, c, False), (init,) * heads)
    carries = step(i, carries, True)
    for h in range(heads):
        _, acc = carries[h]
        o = acc[:C_HEAD_WIDTH, :] * (1.0 / acc[C_HEAD_WIDTH:C_HEAD_WIDTH + 1, :])
        d = o[:, :tq] - lam * o[:, tq:]
        ms = jnp.mean(d * d, axis=0, keepdims=True)
        y = (d * lax.rsqrt(ms + EPS)).T
        y = y * (sg_ref[...] * (1.0 - lam_init))
        sl = slice(h * C_HEAD_WIDTH, (h + 1) * C_HEAD_WIDTH)
        o_ref[:, sl] = (y * cg_ref[:, sl].astype(F32)).astype(BF16)


def _diff(lq, lk, cq, ck, cvt, cg, subln_g, lam_init):
    b, t, _ = cq.shape
    tq = min(DIFF_TQ, t)
    assert tq == DIFF_TK
    width = DIFF_HEADS_PER_STEP * C_HEAD_WIDTH
    qmap = lambda bi, h, i: (bi, i, h)
    kmap = lambda bi, h, i: (bi, 0, h)
    vtmap = lambda bi, h, i: (bi, h, 0)
    const = lambda bi, h, i: (0, 0)
    return pl.pallas_call(
        functools.partial(_diff_kernel, lam_init=lam_init),
        grid=(b, C_HEADS // DIFF_HEADS_PER_STEP, t // tq),
        in_specs=[
            pl.BlockSpec((2, HEAD_DIM), const),
            pl.BlockSpec((2, HEAD_DIM), const),
            pl.BlockSpec((None, tq, width), qmap),
            pl.BlockSpec((None, t, width), kmap),
            pl.BlockSpec((None, width, t), vtmap),
            pl.BlockSpec((None, tq, width), qmap),
            pl.BlockSpec((1, C_HEAD_WIDTH), const),
        ],
        out_specs=pl.BlockSpec((None, tq, width), qmap),
        out_shape=jax.ShapeDtypeStruct((b, t, C_WIDTH), BF16),
        compiler_params=_params(("parallel", "parallel", "arbitrary")),
        name="diff",
    )(lq, lk, cq, ck, cvt, cg, subln_g)


def _merge_kernel(x_ref, ya_ref, yb_ref, yc_ref, mg_ref, wa_ref, wb_ref, wc_ref, wo_ref, o_ref):
    merged = None
    for br, (y_ref, w_ref) in enumerate(((ya_ref, wa_ref), (yb_ref, wb_ref), (yc_ref, wc_ref))):
        proj = jnp.dot(y_ref[...], w_ref[...], preferred_element_type=F32)
        term = mg_ref[:, br * D_MODEL:(br + 1) * D_MODEL].astype(F32) * proj
        merged = term if merged is None else merged + term
    o_ref[...] = x_ref[...] + jnp.dot(merged.astype(BF16), wo_ref[...], preferred_element_type=F32)


def _merge(x2, ya, yb, yc, mg, wa, wb, wc, wo):
    n = x2.shape[0]
    tm = min(ROW_TILE, n)
    row = lambda i: (i, 0)
    const = lambda i: (0, 0)
    return pl.pallas_call(
        _merge_kernel,
        grid=(n // tm,),
        in_specs=[
            pl.BlockSpec((tm, D_MODEL), row),
            pl.BlockSpec((tm, A_WIDTH), row),
            pl.BlockSpec((tm, B_WIDTH), row),
            pl.BlockSpec((tm, C_WIDTH), row),
            pl.BlockSpec((tm, MERGE_COLS), row),
            pl.BlockSpec((A_WIDTH, D_MODEL), const),
            pl.BlockSpec((B_WIDTH, D_MODEL), const),
            pl.BlockSpec((C_WIDTH, D_MODEL), const),
            pl.BlockSpec((D_MODEL, D_MODEL), const),
        ],
        out_specs=pl.BlockSpec((tm, D_MODEL), row),
        out_shape=jax.ShapeDtypeStruct((n, D_MODEL), F32),
        compiler_params=_params(("parallel",)),
        name="merge",
    )(x2, ya, yb, yc, mg, wa, wb, wc, wo)


def kernel(x, norm_g, w_in, attn_q_norm_g, attn_k_norm_g, attn_sinks, w_o_attn, conv_w, conv_b, conv_norm_g, conv_norm_b, w_o_conv, diff_q_norm_g, diff_k_norm_g, lambda_q, lambda_k, diff_subln_g, w_o_diff, w_out):
    b, t, d = x.shape
    n = b * t
    depth = norm_g.shape[0]
    group = lax.broadcasted_iota(jnp.int32, (MXU_COLS, MXU_COLS), 0) // HEAD_DIM
    bd = (group == group.T).astype(BF16)
    tile_gain = lambda g: jnp.tile(g.astype(F32), MXU_COLS // HEAD_DIM).reshape(1, MXU_COLS)
    row = lambda v: v.astype(F32).reshape(1, -1)

    x2 = x.reshape(n, d)
    for l in range(depth):
        w_l = w_in[l].astype(BF16)
        w_k0 = w_l[:, OFF_AK:OFF_AK + HEAD_DIM]
        w_k1 = w_l[:, OFF_AK + HEAD_DIM:OFF_AK + A_KV_WIDTH]
        aq, ak2, avt, ag, u, bg, cq, ck, cvt, cg, mg = _inproj(
            x2, b, row(norm_g[l]), w_l, jnp.concatenate([w_k0, w_k1, w_k1, w_k0], axis=1),
            w_l[:, OFF_AV:OFF_AV + A_KV_WIDTH].T, w_l[:, OFF_CV:OFF_CV + C_WIDTH].T, bd,
            tile_gain(attn_q_norm_g[l]), tile_gain(attn_k_norm_g[l]),
            tile_gain(diff_q_norm_g[l]), tile_gain(diff_k_norm_g[l]))
        r3 = lambda a: a.reshape(b, t, a.shape[-1])
        ya = _swa(attn_sinks[l].astype(F32), r3(aq), r3(ak2), avt, r3(ag))
        yb = _conv(r3(u), conv_w[l].astype(F32), row(conv_b[l]), row(conv_norm_g[l]), row(conv_norm_b[l]), r3(bg))
        lam_init = 0.8 - 0.6 * math.exp(-0.3 * l)
        yc = _diff(lambda_q[l].astype(F32), lambda_k[l].astype(F32), r3(cq), r3(ck), cvt, r3(cg),
                   row(diff_subln_g[l]), lam_init)
        x2 = _merge(x2, ya.reshape(n, -1), yb.reshape(n, -1), yc.reshape(n, -1), mg,
                    w_o_attn[l].astype(BF16), w_o_conv[l].astype(BF16), w_o_diff[l].astype(BF16),
                    w_out[l].astype(BF16))
    return x2.reshape(b, t, d)
```

```python
import functools
import math

import jax
import jax.numpy as jnp
from jax import lax
from jax.experimental import pallas as pl
from jax.experimental.pallas import tpu as pltpu

F32 = jnp.float32
BF16 = jnp.bfloat16

D_MODEL = 1024
HEAD_DIM = 64
EPS = 1e-6
A_Q_HEADS = 8
A_KV_HEADS = 2
A_GROUP = A_Q_HEADS // A_KV_HEADS
A_WIDTH = A_Q_HEADS * HEAD_DIM
A_KV_WIDTH = A_KV_HEADS * HEAD_DIM
WINDOW = 128
B_WIDTH = D_MODEL // 2
CONV_WIDTH = 31
C_HEADS = 4
C_HEAD_WIDTH = 2 * HEAD_DIM
C_WIDTH = C_HEADS * C_HEAD_WIDTH
N_BRANCH = 3
MERGE_COLS = N_BRANCH * D_MODEL

OFF_AQ = 0
OFF_AK = OFF_AQ + A_WIDTH
OFF_AV = OFF_AK + A_KV_WIDTH
OFF_AG = OFF_AV + A_KV_WIDTH
OFF_BGLU = OFF_AG + A_WIDTH
OFF_BG = OFF_BGLU + 2 * B_WIDTH
OFF_CQ = OFF_BG + B_WIDTH
OFF_CK = OFF_CQ + C_WIDTH
OFF_CV = OFF_CK + C_WIDTH
OFF_CG = OFF_CV + C_WIDTH
OFF_MG = OFF_CG + C_WIDTH
IN_WIDTH = OFF_MG + MERGE_COLS

MXU_COLS = 256
LANES = 128
SUBLANES = 8
VMEM_LIMIT_BYTES = 56 * 1024 * 1024

ROW_TILE = 512
SWA_TILE = 1024
CONV_TILE = 512
CONV_HALO = 32
CONV_CHUNK = 64
DIFF_TQ = 512
DIFF_TK = 512
DIFF_HEADS_PER_STEP = 2
DIFF_SUM_ROWS = 16


def _sigmoid(y):
    return 1.0 / (1.0 + jnp.exp(-y))


def _silu(y):
    return y * _sigmoid(y)


def _params(semantics):
    return pltpu.CompilerParams(dimension_semantics=semantics, vmem_limit_bytes=VMEM_LIMIT_BYTES)


def _inproj_kernel(x_ref, g_ref, w_ref, wak2_ref, wavt_ref, wcvt_ref, bd_ref, aqg_ref, akg_ref, cqg_ref, ckg_ref,
                   aq_o, ak2_o, avt_o, ag_o, u_o, bg_o, cq_o, ck_o, cvt_o, cg_o, mg_o):
    x = x_ref[...]
    ms = jnp.mean(x * x, axis=-1, keepdims=True)
    h = (x * lax.rsqrt(ms + EPS) * g_ref[...]).astype(BF16)

    def proj(off, width):
        return jnp.dot(h, w_ref[:, off:off + width], preferred_element_type=F32)

    def head_norm(y, gain, scale):
        w = y.shape[-1]
        sq = (y * y).astype(BF16)
        ssum = jnp.dot(sq, bd_ref[:w, :w], preferred_element_type=F32)
        return y * lax.rsqrt(ssum * (1.0 / HEAD_DIM) + EPS) * (gain * scale)

    qscale_log2 = HEAD_DIM ** -0.5 * math.log2(math.e)
    nt = (((1,), (1,)), ((), ()))
    for c in range(A_WIDTH // MXU_COLS):
        sl = slice(c * MXU_COLS, (c + 1) * MXU_COLS)
        aq_o[:, sl] = head_norm(proj(OFF_AQ + c * MXU_COLS, MXU_COLS), aqg_ref[...], qscale_log2).astype(BF16)
        ag_o[:, sl] = _silu(proj(OFF_AG + c * MXU_COLS, MXU_COLS)).astype(BF16)
    ak2 = jnp.dot(h, wak2_ref[...], preferred_element_type=F32)
    ak2_o[...] = head_norm(ak2, akg_ref[...], 1.0).astype(BF16)
    avt_o[...] = lax.dot_general(wavt_ref[...], h, nt, preferred_element_type=F32).astype(BF16)
    for c in range(B_WIDTH // MXU_COLS):
        sl = slice(c * MXU_COLS, (c + 1) * MXU_COLS)
        a = proj(OFF_BGLU + c * MXU_COLS, MXU_COLS)
        b = proj(OFF_BGLU + B_WIDTH + c * MXU_COLS, MXU_COLS)
        u_o[:, sl] = (a * _sigmoid(b)).astype(BF16)
        bg_o[:, sl] = _silu(proj(OFF_BG + c * MXU_COLS, MXU_COLS)).astype(BF16)
    for c in range(C_WIDTH // MXU_COLS):
        sl = slice(c * MXU_COLS, (c + 1) * MXU_COLS)
        cq_o[:, sl] = head_norm(proj(OFF_CQ + c * MXU_COLS, MXU_COLS), cqg_ref[...], qscale_log2).astype(BF16)
        ck_o[:, sl] = head_norm(proj(OFF_CK + c * MXU_COLS, MXU_COLS), ckg_ref[...], 1.0).astype(BF16)
        cg_o[:, sl] = _silu(proj(OFF_CG + c * MXU_COLS, MXU_COLS)).astype(BF16)
    cvt_o[...] = lax.dot_general(wcvt_ref[...], h, nt, preferred_element_type=F32).astype(BF16)
    for c in range(MERGE_COLS // MXU_COLS):
        sl = slice(c * MXU_COLS, (c + 1) * MXU_COLS)
        mg_o[:, sl] = _sigmoid(proj(OFF_MG + c * MXU_COLS, MXU_COLS)).astype(BF16)


def _inproj(x2, batch, layer, norm_g, w_in, w_ak2, w_avt, w_cvt, bd, aqg, akg, cqg, ckg):
    n = x2.shape[0]
    t = n // batch
    tm = min(ROW_TILE, t)
    per = t // tm
    const = lambda i: (0, 0)
    row = lambda i: (i, 0)
    rows = lambda w: (pl.BlockSpec((tm, w), row), jax.ShapeDtypeStruct((n, w), BF16))
    cols = lambda w: (pl.BlockSpec((None, w, tm), lambda i: (i // per, 0, i % per)),
                      jax.ShapeDtypeStruct((batch, w, t), BF16))
    outs = (rows(A_WIDTH), rows(2 * A_KV_WIDTH), cols(A_KV_WIDTH), rows(A_WIDTH), rows(B_WIDTH), rows(B_WIDTH),
            rows(C_WIDTH), rows(C_WIDTH), cols(C_WIDTH), rows(C_WIDTH), rows(MERGE_COLS))
    resident = lambda shape: pl.BlockSpec(shape, const, pipeline_mode=pl.Buffered(1))
    return pl.pallas_call(
        _inproj_kernel,
        grid=(n // tm,),
        in_specs=[
            pl.BlockSpec((tm, D_MODEL), row),
            pl.BlockSpec((1, D_MODEL), const),
            pl.BlockSpec((None, D_MODEL, IN_WIDTH), lambda i: (layer, 0, 0), pipeline_mode=pl.Buffered(1)),
            resident((D_MODEL, 2 * A_KV_WIDTH)),
            resident((A_KV_WIDTH, D_MODEL)),
            resident((C_WIDTH, D_MODEL)),
            pl.BlockSpec((MXU_COLS, MXU_COLS), const),
            pl.BlockSpec((1, MXU_COLS), const),
            pl.BlockSpec((1, MXU_COLS), const),
            pl.BlockSpec((1, MXU_COLS), const),
            pl.BlockSpec((1, MXU_COLS), const),
        ],
        out_specs=[o[0] for o in outs],
        out_shape=[o[1] for o in outs],
        compiler_params=_params(("parallel",)),
        name="inproj",
    )(x2, norm_g, w_in, w_ak2, w_avt, w_cvt, bd, aqg, akg, cqg, ckg)


def _swa_kernel(sink_ref, q_ref, kc_ref, kp_ref, vtc_ref, vtp_ref, ag_ref, o_ref):
    i = pl.program_id(1)
    nblk = q_ref.shape[0] // WINDOW
    kfull = jnp.concatenate([kp_ref[...], kc_ref[...]], axis=0)
    vtfull = jnp.concatenate([vtp_ref[...], vtc_ref[...]], axis=1)
    pair = 2 * WINDOW
    c = lax.broadcasted_iota(jnp.int32, (pair, pair), 0)
    t = lax.broadcasted_iota(jnp.int32, (pair, pair), 1) & (WINDOW - 1)
    local = (c > t) & (c <= t + WINDOW)
    first_col = lax.broadcasted_iota(jnp.int32, (1, pair), 1) < WINDOW
    half = lax.broadcasted_iota(jnp.int32, (WINDOW, LANES), 1) < HEAD_DIM
    ones_rows = jnp.ones((DIFF_SUM_ROWS, pair), BF16)
    log2e = math.log2(math.e)
    chains = [(j, kv, par) for j in range(nblk) for kv in range(A_KV_HEADS) for par in range(2)]

    def scores(j, kv, par):
        q = q_ref[j * WINDOW:(j + 1) * WINDOW, :]
        kwin = kfull[j * WINDOW:(j + 2) * WINDOW, :]
        qz = []
        for h in (kv * A_GROUP + par, kv * A_GROUP + par + 2):
            blk = q[:, (h // 2) * LANES:(h // 2 + 1) * LANES]
            keep = half if par == 0 else jnp.logical_not(half)
            qz.append(jnp.where(keep, blk, jnp.zeros_like(blk)))
        qz = jnp.concatenate(qz, axis=0)
        ksel = kwin[:, :LANES] if par == kv else kwin[:, LANES:]
        return lax.dot_general(ksel, qz, (((1,), (1,)), ((), ())), preferred_element_type=F32)

    s_all = [scores(*ch) for ch in chains]
    pieces = {}
    for (j, kv, par), s in zip(chains, s_all):
        heads = (kv * A_GROUP + par, kv * A_GROUP + par + 2)
        mask = local & ((c >= WINDOW) | (i > 0)) if j == 0 else local
        s = jnp.where(mask, s, -jnp.inf)
        sink = jnp.where(first_col, sink_ref[heads[0]] * log2e, sink_ref[heads[1]] * log2e)
        m = jnp.maximum(jnp.max(s, axis=0, keepdims=True), sink)
        p = jnp.exp2(s - m).astype(BF16)
        vtwin = vtfull[kv * HEAD_DIM:(kv + 1) * HEAD_DIM, j * WINDOW:(j + 2) * WINDOW]
        acc = jnp.dot(jnp.concatenate([vtwin, ones_rows], axis=0), p, preferred_element_type=F32)
        denom = acc[HEAD_DIM:HEAD_DIM + 1, :] + jnp.exp2(sink - m)
        o = acc[:HEAD_DIM, :] * (1.0 / denom)
        pieces[j, heads[0]] = o[:, :WINDOW]
        pieces[j, heads[1]] = o[:, WINDOW:]
    for j in range(nblk):
        cols = [jnp.concatenate([pieces[j, 2 * cb], pieces[j, 2 * cb + 1]], axis=0).T for cb in range(A_Q_HEADS // 2)]
        o_all = jnp.concatenate(cols, axis=1)
        gate = ag_ref[j * WINDOW:(j + 1) * WINDOW, :].astype(F32)
        o_ref[j * WINDOW:(j + 1) * WINDOW, :] = (o_all * gate).astype(BF16)


def _swa(sinks, aq, ak2, avt, ag):
    b, t, _ = aq.shape
    tq = min(SWA_TILE, t)
    per = tq // WINDOW
    cur = lambda bi, i: (bi, i, 0)
    prev = lambda bi, i: (bi, jnp.maximum(i * per - 1, 0), 0)
    cur_t = lambda bi, i: (bi, 0, i)
    prev_t = lambda bi, i: (bi, 0, jnp.maximum(i * per - 1, 0))
    return pl.pallas_call(
        _swa_kernel,
        grid=(b, t // tq),
        in_specs=[
            pl.BlockSpec(memory_space=pltpu.SMEM),
            pl.BlockSpec((None, tq, A_WIDTH), cur),
            pl.BlockSpec((None, tq, 2 * A_KV_WIDTH), cur),
            pl.BlockSpec((None, WINDOW, 2 * A_KV_WIDTH), prev),
            pl.BlockSpec((None, A_KV_WIDTH, tq), cur_t),
            pl.BlockSpec((None, A_KV_WIDTH, WINDOW), prev_t),
            pl.BlockSpec((None, tq, A_WIDTH), cur),
        ],
        out_specs=pl.BlockSpec((None, tq, A_WIDTH), cur),
        out_shape=jax.ShapeDtypeStruct((b, t, A_WIDTH), BF16),
        compiler_params=_params(("parallel", "parallel")),
        name="swa",
    )(sinks, aq, ak2, ak2, avt, avt, ag)


def _conv_kernel(uc_ref, up_ref, w_ref, cb_ref, lg_ref, lb_ref, bg_ref, o_ref, ubuf):
    i = pl.program_id(1)
    tt = uc_ref.shape[0]
    halo = up_ref[...].astype(F32)
    ubuf[0, 0:CONV_HALO, :] = jnp.where(i > 0, halo, 0.0)
    ubuf[0, CONV_HALO:, :] = uc_ref[...].astype(F32)
    shifted_rows = tt + CONV_HALO - SUBLANES
    for s in range(1, SUBLANES):
        ubuf[s, 0:shifted_rows, :] = ubuf[0, s:s + shifted_rows, :]
    first = CONV_HALO - (CONV_WIDTH - 1)
    for r0 in range(0, tt, CONV_CHUNK):
        acc = jnp.zeros((CONV_CHUNK, B_WIDTH), F32)
        for k in range(CONV_WIDTH):
            shift = (first + k) % SUBLANES
            base = r0 + first + k - shift
            acc = acc + w_ref[k:k + 1, :] * ubuf[shift, base:base + CONV_CHUNK, :]
        y = acc + cb_ref[...]
        mu = jnp.mean(y, axis=-1, keepdims=True)
        yc = y - mu
        var = jnp.mean(yc * yc, axis=-1, keepdims=True)
        yn = yc * lax.rsqrt(var + EPS) * lg_ref[...] + lb_ref[...]
        gate = bg_ref[r0:r0 + CONV_CHUNK, :].astype(F32)
        o_ref[r0:r0 + CONV_CHUNK, :] = (_silu(yn) * gate).astype(BF16)


def _conv(u, conv_w, conv_b, ln_g, ln_b, bg):
    b, t, _ = u.shape
    tt = min(CONV_TILE, t)
    per = tt // CONV_HALO
    cur = lambda bi, i: (bi, i, 0)
    prev = lambda bi, i: (bi, jnp.maximum(i * per - 1, 0), 0)
    const = lambda bi, i: (0, 0)
    return pl.pallas_call(
        _conv_kernel,
        grid=(b, t // tt),
        in_specs=[
            pl.BlockSpec((None, tt, B_WIDTH), cur),
            pl.BlockSpec((None, CONV_HALO, B_WIDTH), prev),
            pl.BlockSpec((CONV_WIDTH, B_WIDTH), const),
            pl.BlockSpec((1, B_WIDTH), const),
            pl.BlockSpec((1, B_WIDTH), const),
            pl.BlockSpec((1, B_WIDTH), const),
            pl.BlockSpec((None, tt, B_WIDTH), cur),
        ],
        out_specs=pl.BlockSpec((None, tt, B_WIDTH), cur),
        out_shape=jax.ShapeDtypeStruct((b, t, B_WIDTH), BF16),
        scratch_shapes=[pltpu.VMEM((SUBLANES, tt + CONV_HALO, B_WIDTH), F32)],
        compiler_params=_params(("parallel", "parallel")),
        name="conv",
    )(u, u, conv_w, conv_b, ln_g, ln_b, bg)


def _diff_kernel(lq_ref, lk_ref, q_ref, k_ref, vt_ref, cg_ref, sg_ref, o_ref, *, lam_init):
    i = pl.program_id(2)
    tq = q_ref.shape[0]
    tk = DIFF_TK
    heads = q_ref.shape[1] // C_HEAD_WIDTH
    e = jnp.exp(jnp.sum(lq_ref[...] * lk_ref[...], axis=-1, keepdims=True))
    lam = e[0:1, :] - e[1:2, :] + lam_init

    lane = lax.broadcasted_iota(jnp.int32, (tq, C_HEAD_WIDTH), 1)
    qs = []
    for h in range(heads):
        q = q_ref[:, h * C_HEAD_WIDTH:(h + 1) * C_HEAD_WIDTH]
        zero = jnp.zeros_like(q)
        qs.append(jnp.concatenate([jnp.where(lane < HEAD_DIM, q, zero), jnp.where(lane >= HEAD_DIM, q, zero)],
                                  axis=0))

    def scores(h, start, size):
        kt = k_ref[pl.ds(start, size), h * C_HEAD_WIDTH:(h + 1) * C_HEAD_WIDTH]
        return lax.dot_general(kt, qs[h], (((1,), (1,)), ((), ())), preferred_element_type=F32)

    def head_step(h, start, size, s, carry, masked):
        m, acc = carry
        vt = vt_ref[h * C_HEAD_WIDTH:(h + 1) * C_HEAD_WIDTH, pl.ds(start, size)]
        vt = jnp.concatenate([vt, jnp.ones((DIFF_SUM_ROWS, size), BF16)], axis=0)
        if masked:
            kpos = lax.broadcasted_iota(jnp.int32, s.shape, 0)
            qpos = lax.broadcasted_iota(jnp.int32, s.shape, 1) & (tq - 1)
            s = jnp.where(kpos <= qpos, s, -jnp.inf)
        m_new = jnp.maximum(m, jnp.max(s, axis=0, keepdims=True))
        alpha = jnp.exp2(m - m_new)
        p = jnp.exp2(s - m_new).astype(BF16)
        acc = alpha * acc + jnp.dot(vt, p, preferred_element_type=F32)
        return m_new, acc

    def step(start, size, carries, masked):
        s_all = [scores(h, start, size) for h in range(heads)]
        return tuple(head_step(h, start, size, s_all[h], carries[h], masked) for h in range(heads))

    init = (jnp.full((1, 2 * tq), -jnp.inf, F32), jnp.zeros((C_HEAD_WIDTH + DIFF_SUM_ROWS, 2 * tq), F32))
    carries = lax.fori_loop(
        0, i // 2, lambda jj, c: step(pl.multiple_of(jj * (2 * tk), 2 * tk), 2 * tk, c, False), (init,) * heads)
    carries = lax.cond(i % 2 == 1, lambda c: step(pl.multiple_of((i - 1) * tk, tk), tk, c, False), lambda c: c,
                       carries)
    carries = step(pl.multiple_of(i * tk, tk), tk, carries, True)
    for h in range(heads):
        _, acc = carries[h]
        o = acc[:C_HEAD_WIDTH, :] * (1.0 / acc[C_HEAD_WIDTH:C_HEAD_WIDTH + 1, :])
        d = o[:, :tq] - lam * o[:, tq:]
        ms = jnp.mean(d * d, axis=0, keepdims=True)
        y = (d * lax.rsqrt(ms + EPS)).T
        y = y * (sg_ref[...] * (1.0 - lam_init))
        sl = slice(h * C_HEAD_WIDTH, (h + 1) * C_HEAD_WIDTH)
        o_ref[:, sl] = (y * cg_ref[:, sl].astype(F32)).astype(BF16)


def _diff(lq, lk, cq, ck, cvt, cg, subln_g, lam_init):
    b, t, _ = cq.shape
    tq = min(DIFF_TQ, t)
    assert tq == DIFF_TK
    width = DIFF_HEADS_PER_STEP * C_HEAD_WIDTH
    qmap = lambda bi, h, i: (bi, i, h)
    kmap = lambda bi, h, i: (bi, 0, h)
    vtmap = lambda bi, h, i: (bi, h, 0)
    const = lambda bi, h, i: (0, 0)
    return pl.pallas_call(
        functools.partial(_diff_kernel, lam_init=lam_init),
        grid=(b, C_HEADS // DIFF_HEADS_PER_STEP, t // tq),
        in_specs=[
            pl.BlockSpec((2, HEAD_DIM), const),
            pl.BlockSpec((2, HEAD_DIM), const),
            pl.BlockSpec((None, tq, width), qmap),
            pl.BlockSpec((None, t, width), kmap),
            pl.BlockSpec((None, width, t), vtmap),
            pl.BlockSpec((None, tq, width), qmap),
            pl.BlockSpec((1, C_HEAD_WIDTH), const),
        ],
        out_specs=pl.BlockSpec((None, tq, width), qmap),
        out_shape=jax.ShapeDtypeStruct((b, t, C_WIDTH), BF16),
        compiler_params=_params(("parallel", "parallel", "arbitrary")),
        name="diff",
    )(lq, lk, cq, ck, cvt, cg, subln_g)


def _merge_kernel(x_ref, ya_ref, yb_ref, yc_ref, mg_ref, wa_ref, wb_ref, wc_ref, wo_ref, o_ref):
    merged = None
    for br, (y_ref, w_ref) in enumerate(((ya_ref, wa_ref), (yb_ref, wb_ref), (yc_ref, wc_ref))):
        proj = jnp.dot(y_ref[...], w_ref[...], preferred_element_type=F32)
        term = mg_ref[:, br * D_MODEL:(br + 1) * D_MODEL].astype(F32) * proj
        merged = term if merged is None else merged + term
    o_ref[...] = x_ref[...] + jnp.dot(merged.astype(BF16), wo_ref[...], preferred_element_type=F32)


def _merge(x2, ya, yb, yc, mg, wa, wb, wc, wo):
    n = x2.shape[0]
    tm = min(ROW_TILE, n)
    row = lambda i: (i, 0)
    const = lambda i: (0, 0)
    return pl.pallas_call(
        _merge_kernel,
        grid=(n // tm,),
        in_specs=[
            pl.BlockSpec((tm, D_MODEL), row),
            pl.BlockSpec((tm, A_WIDTH), row),
            pl.BlockSpec((tm, B_WIDTH), row),
            pl.BlockSpec((tm, C_WIDTH), row),
            pl.BlockSpec((tm, MERGE_COLS), row),
            pl.BlockSpec((A_WIDTH, D_MODEL), const),
            pl.BlockSpec((B_WIDTH, D_MODEL), const),
            pl.BlockSpec((C_WIDTH, D_MODEL), const),
            pl.BlockSpec((D_MODEL, D_MODEL), const),
        ],
        out_specs=pl.BlockSpec((tm, D_MODEL), row),
        out_shape=jax.ShapeDtypeStruct((n, D_MODEL), F32),
        compiler_params=_params(("parallel",)),
        name="merge",
    )(x2, ya, yb, yc, mg, wa, wb, wc, wo)


def kernel(x, norm_g, w_in, attn_q_norm_g, attn_k_norm_g, attn_sinks, w_o_attn, conv_w, conv_b, conv_norm_g, conv_norm_b, w_o_conv, diff_q_norm_g, diff_k_norm_g, lambda_q, lambda_k, diff_subln_g, w_o_diff, w_out):
    b, t, d = x.shape
    n = b * t
    depth = norm_g.shape[0]
    group = lax.broadcasted_iota(jnp.int32, (MXU_COLS, MXU_COLS), 0) // HEAD_DIM
    bd = (group == group.T).astype(BF16)
    tile_gain = lambda g: jnp.tile(g.astype(F32), MXU_COLS // HEAD_DIM).reshape(1, MXU_COLS)
    row = lambda v: v.astype(F32).reshape(1, -1)

    x2 = x.reshape(n, d)
    w_bf = w_in.astype(BF16)
    for l in range(depth):
        w_k0 = w_bf[l, :, OFF_AK:OFF_AK + HEAD_DIM]
        w_k1 = w_bf[l, :, OFF_AK + HEAD_DIM:OFF_AK + A_KV_WIDTH]
        aq, ak2, avt, ag, u, bg, cq, ck, cvt, cg, mg = _inproj(
            x2, b, l, row(norm_g[l]), w_bf, jnp.concatenate([w_k0, w_k1, w_k1, w_k0], axis=1),
            w_bf[l, :, OFF_AV:OFF_AV + A_KV_WIDTH].T, w_bf[l, :, OFF_CV:OFF_CV + C_WIDTH].T, bd,
            tile_gain(attn_q_norm_g[l]), tile_gain(attn_k_norm_g[l]),
            tile_gain(diff_q_norm_g[l]), tile_gain(diff_k_norm_g[l]))
        r3 = lambda a: a.reshape(b, t, a.shape[-1])
        ya = _swa(attn_sinks[l].astype(F32), r3(aq), r3(ak2), avt, r3(ag))
        yb = _conv(r3(u), conv_w[l].astype(F32), row(conv_b[l]), row(conv_norm_g[l]), row(conv_norm_b[l]), r3(bg))
        lam_init = 0.8 - 0.6 * math.exp(-0.3 * l)
        yc = _diff(lambda_q[l].astype(F32), lambda_k[l].astype(F32), r3(cq), r3(ck), cvt, r3(cg),
                   row(diff_subln_g[l]), lam_init)
        x2 = _merge(x2, ya.reshape(n, -1), yb.reshape(n, -1), yc.reshape(n, -1), mg,
                    w_o_attn[l].astype(BF16), w_o_conv[l].astype(BF16), w_o_diff[l].astype(BF16),
                    w_out[l].astype(BF16))
    return x2.reshape(b, t, d)
```

```python
import functools
import math

import jax
import jax.numpy as jnp
from jax import lax
from jax.experimental import pallas as pl
from jax.experimental.pallas import tpu as pltpu

F32 = jnp.float32
BF16 = jnp.bfloat16

D_MODEL = 1024
HEAD_DIM = 64
EPS = 1e-6
A_Q_HEADS = 8
A_KV_HEADS = 2
A_GROUP = A_Q_HEADS // A_KV_HEADS
A_WIDTH = A_Q_HEADS * HEAD_DIM
A_KV_WIDTH = A_KV_HEADS * HEAD_DIM
WINDOW = 128
B_WIDTH = D_MODEL // 2
CONV_WIDTH = 31
C_HEADS = 4
C_HEAD_WIDTH = 2 * HEAD_DIM
C_WIDTH = C_HEADS * C_HEAD_WIDTH
N_BRANCH = 3
MERGE_COLS = N_BRANCH * D_MODEL

OFF_AQ = 0
OFF_AK = OFF_AQ + A_WIDTH
OFF_AV = OFF_AK + A_KV_WIDTH
OFF_AG = OFF_AV + A_KV_WIDTH
OFF_BGLU = OFF_AG + A_WIDTH
OFF_BG = OFF_BGLU + 2 * B_WIDTH
OFF_CQ = OFF_BG + B_WIDTH
OFF_CK = OFF_CQ + C_WIDTH
OFF_CV = OFF_CK + C_WIDTH
OFF_CG = OFF_CV + C_WIDTH
OFF_MG = OFF_CG + C_WIDTH
IN_WIDTH = OFF_MG + MERGE_COLS

MXU_COLS = 256
LANES = 128
SUBLANES = 8
VMEM_LIMIT_BYTES = 56 * 1024 * 1024

ROW_TILE = 512
CAST_ROWS = 256
SWA_TILE = 1024
CONV_HALO = 32
CONV_CHUNK = 64
DIFF_TQ = 512
DIFF_TK = 512
DIFF_HEADS_PER_STEP = 2
DIFF_SUM_ROWS = 16


def _sigmoid(y):
    return 1.0 / (1.0 + jnp.exp(-y))


def _silu(y):
    return y * _sigmoid(y)


def _params(semantics):
    return pltpu.CompilerParams(dimension_semantics=semantics, vmem_limit_bytes=VMEM_LIMIT_BYTES)


def _conv_module(ubuf, gbuf, w_ref, cb_ref, lg_ref, lb_ref, o_ref):
    tt = o_ref.shape[0]
    shifted_rows = tt + CONV_HALO - SUBLANES
    for s in range(1, SUBLANES):
        ubuf[s, 0:shifted_rows, :] = ubuf[0, s:s + shifted_rows, :]
    first = CONV_HALO - (CONV_WIDTH - 1)
    for r0 in range(0, tt, CONV_CHUNK):
        acc = jnp.zeros((CONV_CHUNK, B_WIDTH), F32)
        for k in range(CONV_WIDTH):
            shift = (first + k) % SUBLANES
            base = r0 + first + k - shift
            acc = acc + w_ref[k:k + 1, :] * ubuf[shift, base:base + CONV_CHUNK, :]
        y = acc + cb_ref[...]
        mu = jnp.mean(y, axis=-1, keepdims=True)
        yc = y - mu
        var = jnp.mean(yc * yc, axis=-1, keepdims=True)
        yn = yc * lax.rsqrt(var + EPS) * lg_ref[...] + lb_ref[...]
        o_ref[r0:r0 + CONV_CHUNK, :] = (_silu(yn) * gbuf[r0:r0 + CONV_CHUNK, :]).astype(BF16)


def _inproj_kernel(x_ref, g_ref, w_ref, wak2_ref, wavt_ref, wcvt_ref, bd_ref, aqg_ref, akg_ref, cqg_ref, ckg_ref,
                   cw_ref, cb_ref, lg_ref, lb_ref,
                   aq_o, ak2_o, avt_o, ag_o, yb_o, cq_o, ck_o, cvt_o, cg_o, mg_o, ubuf, gbuf, *, tiles_per_seq):
    tm = x_ref.shape[0]
    x = x_ref[...]
    ms = jnp.mean(x * x, axis=-1, keepdims=True)
    h = (x * lax.rsqrt(ms + EPS) * g_ref[...]).astype(BF16)

    def proj(off, width):
        return jnp.dot(h, w_ref[:, off:off + width], preferred_element_type=F32)

    def head_norm(y, gain, scale):
        w = y.shape[-1]
        sq = (y * y).astype(BF16)
        ssum = jnp.dot(sq, bd_ref[:w, :w], preferred_element_type=F32)
        return y * lax.rsqrt(ssum * (1.0 / HEAD_DIM) + EPS) * (gain * scale)

    qscale_log2 = HEAD_DIM ** -0.5 * math.log2(math.e)
    nt = (((1,), (1,)), ((), ()))
    for c in range(A_WIDTH // MXU_COLS):
        sl = slice(c * MXU_COLS, (c + 1) * MXU_COLS)
        aq_o[:, sl] = head_norm(proj(OFF_AQ + c * MXU_COLS, MXU_COLS), aqg_ref[...], qscale_log2).astype(BF16)
        ag_o[:, sl] = _silu(proj(OFF_AG + c * MXU_COLS, MXU_COLS)).astype(BF16)
    ak2 = jnp.dot(h, wak2_ref[...], preferred_element_type=F32)
    ak2_o[...] = head_norm(ak2, akg_ref[...], 1.0).astype(BF16)
    avt_o[...] = lax.dot_general(wavt_ref[...], h, nt, preferred_element_type=F32).astype(BF16)
    seq_tile = pl.program_id(0) % tiles_per_seq

    @pl.when(seq_tile == 0)
    def _():
        ubuf[0, 0:CONV_HALO, :] = jnp.zeros((CONV_HALO, B_WIDTH), F32)

    @pl.when(seq_tile > 0)
    def _():
        ubuf[0, 0:CONV_HALO, :] = ubuf[0, tm:tm + CONV_HALO, :]

    for c in range(B_WIDTH // MXU_COLS):
        sl = slice(c * MXU_COLS, (c + 1) * MXU_COLS)
        a = proj(OFF_BGLU + c * MXU_COLS, MXU_COLS)
        b = proj(OFF_BGLU + B_WIDTH + c * MXU_COLS, MXU_COLS)
        ubuf[0, CONV_HALO:, sl] = a * _sigmoid(b)
        gbuf[:, sl] = _silu(proj(OFF_BG + c * MXU_COLS, MXU_COLS))
    _conv_module(ubuf, gbuf, cw_ref, cb_ref, lg_ref, lb_ref, yb_o)
    for c in range(C_WIDTH // MXU_COLS):
        sl = slice(c * MXU_COLS, (c + 1) * MXU_COLS)
        cq_o[:, sl] = head_norm(proj(OFF_CQ + c * MXU_COLS, MXU_COLS), cqg_ref[...], qscale_log2).astype(BF16)
        ck_o[:, sl] = head_norm(proj(OFF_CK + c * MXU_COLS, MXU_COLS), ckg_ref[...], 1.0).astype(BF16)
        cg_o[:, sl] = _silu(proj(OFF_CG + c * MXU_COLS, MXU_COLS)).astype(BF16)
    cvt_o[...] = lax.dot_general(wcvt_ref[...], h, nt, preferred_element_type=F32).astype(BF16)
    for c in range(MERGE_COLS // MXU_COLS):
        sl = slice(c * MXU_COLS, (c + 1) * MXU_COLS)
        mg_o[:, sl] = _sigmoid(proj(OFF_MG + c * MXU_COLS, MXU_COLS)).astype(BF16)


def _inproj(x2, batch, layer, norm_g, w_in, w_ak2, w_avt, w_cvt, bd, aqg, akg, cqg, ckg, conv_w, conv_b, ln_g, ln_b):
    n = x2.shape[0]
    t = n // batch
    tm = min(ROW_TILE, t)
    per = t // tm
    const = lambda i: (0, 0)
    row = lambda i: (i, 0)
    rows = lambda w: (pl.BlockSpec((tm, w), row), jax.ShapeDtypeStruct((n, w), BF16))
    cols = lambda w: (pl.BlockSpec((None, w, tm), lambda i: (i // per, 0, i % per)),
                      jax.ShapeDtypeStruct((batch, w, t), BF16))
    outs = (rows(A_WIDTH), rows(2 * A_KV_WIDTH), cols(A_KV_WIDTH), rows(A_WIDTH), rows(B_WIDTH),
            rows(C_WIDTH), rows(C_WIDTH), cols(C_WIDTH), rows(C_WIDTH), rows(MERGE_COLS))
    resident = lambda shape: pl.BlockSpec(shape, const, pipeline_mode=pl.Buffered(1))
    return pl.pallas_call(
        functools.partial(_inproj_kernel, tiles_per_seq=per),
        grid=(n // tm,),
        in_specs=[
            pl.BlockSpec((tm, D_MODEL), row),
            pl.BlockSpec((1, D_MODEL), const),
            pl.BlockSpec((None, D_MODEL, IN_WIDTH), lambda i: (layer, 0, 0), pipeline_mode=pl.Buffered(1)),
            resident((D_MODEL, 2 * A_KV_WIDTH)),
            resident((A_KV_WIDTH, D_MODEL)),
            resident((C_WIDTH, D_MODEL)),
            pl.BlockSpec((MXU_COLS, MXU_COLS), const),
            pl.BlockSpec((1, MXU_COLS), const),
            pl.BlockSpec((1, MXU_COLS), const),
            pl.BlockSpec((1, MXU_COLS), const),
            pl.BlockSpec((1, MXU_COLS), const),
            pl.BlockSpec((CONV_WIDTH, B_WIDTH), const),
            pl.BlockSpec((1, B_WIDTH), const),
            pl.BlockSpec((1, B_WIDTH), const),
            pl.BlockSpec((1, B_WIDTH), const),
        ],
        out_specs=[o[0] for o in outs],
        out_shape=[o[1] for o in outs],
        scratch_shapes=[pltpu.VMEM((SUBLANES, tm + CONV_HALO, B_WIDTH), F32), pltpu.VMEM((tm, B_WIDTH), F32)],
        compiler_params=_params(("arbitrary",)),
        name="inproj",
    )(x2, norm_g, w_in, w_ak2, w_avt, w_cvt, bd, aqg, akg, cqg, ckg, conv_w, conv_b, ln_g, ln_b)


def _swa_kernel(sink_ref, q_ref, kc_ref, kp_ref, vtc_ref, vtp_ref, ag_ref, o_ref):
    i = pl.program_id(1)
    nblk = q_ref.shape[0] // WINDOW
    kfull = jnp.concatenate([kp_ref[...], kc_ref[...]], axis=0)
    vtfull = jnp.concatenate([vtp_ref[...], vtc_ref[...]], axis=1)
    pair = 2 * WINDOW
    c = lax.broadcasted_iota(jnp.int32, (pair, pair), 0)
    t = lax.broadcasted_iota(jnp.int32, (pair, pair), 1) & (WINDOW - 1)
    local = (c > t) & (c <= t + WINDOW)
    first_col = lax.broadcasted_iota(jnp.int32, (1, pair), 1) < WINDOW
    half = lax.broadcasted_iota(jnp.int32, (WINDOW, LANES), 1) < HEAD_DIM
    ones_rows = jnp.ones((DIFF_SUM_ROWS, pair), BF16)
    log2e = math.log2(math.e)
    chains = [(j, kv, par) for j in range(nblk) for kv in range(A_KV_HEADS) for par in range(2)]

    def scores(j, kv, par):
        q = q_ref[j * WINDOW:(j + 1) * WINDOW, :]
        kwin = kfull[j * WINDOW:(j + 2) * WINDOW, :]
        qz = []
        for h in (kv * A_GROUP + par, kv * A_GROUP + par + 2):
            blk = q[:, (h // 2) * LANES:(h // 2 + 1) * LANES]
            keep = half if par == 0 else jnp.logical_not(half)
            qz.append(jnp.where(keep, blk, jnp.zeros_like(blk)))
        qz = jnp.concatenate(qz, axis=0)
        ksel = kwin[:, :LANES] if par == kv else kwin[:, LANES:]
        return lax.dot_general(ksel, qz, (((1,), (1,)), ((), ())), preferred_element_type=F32)

    s_all = [scores(*ch) for ch in chains]
    pieces = {}
    for (j, kv, par), s in zip(chains, s_all):
        heads = (kv * A_GROUP + par, kv * A_GROUP + par + 2)
        mask = local & ((c >= WINDOW) | (i > 0)) if j == 0 else local
        s = jnp.where(mask, s, -jnp.inf)
        sink = jnp.where(first_col, sink_ref[heads[0]] * log2e, sink_ref[heads[1]] * log2e)
        m = jnp.maximum(jnp.max(s, axis=0, keepdims=True), sink)
        p = jnp.exp2(s - m).astype(BF16)
        vtwin = vtfull[kv * HEAD_DIM:(kv + 1) * HEAD_DIM, j * WINDOW:(j + 2) * WINDOW]
        acc = jnp.dot(jnp.concatenate([vtwin, ones_rows], axis=0), p, preferred_element_type=F32)
        denom = acc[HEAD_DIM:HEAD_DIM + 1, :] + jnp.exp2(sink - m)
        o = acc[:HEAD_DIM, :] * (1.0 / denom)
        pieces[j, heads[0]] = o[:, :WINDOW]
        pieces[j, heads[1]] = o[:, WINDOW:]
    for j in range(nblk):
        cols = [jnp.concatenate([pieces[j, 2 * cb], pieces[j, 2 * cb + 1]], axis=0).T for cb in range(A_Q_HEADS // 2)]
        o_all = jnp.concatenate(cols, axis=1)
        gate = ag_ref[j * WINDOW:(j + 1) * WINDOW, :].astype(F32)
        o_ref[j * WINDOW:(j + 1) * WINDOW, :] = (o_all * gate).astype(BF16)


def _swa(sinks, aq, ak2, avt, ag):
    b, t, _ = aq.shape
    tq = min(SWA_TILE, t)
    per = tq // WINDOW
    cur = lambda bi, i: (bi, i, 0)
    prev = lambda bi, i: (bi, jnp.maximum(i * per - 1, 0), 0)
    cur_t = lambda bi, i: (bi, 0, i)
    prev_t = lambda bi, i: (bi, 0, jnp.maximum(i * per - 1, 0))
    return pl.pallas_call(
        _swa_kernel,
        grid=(b, t // tq),
        in_specs=[
            pl.BlockSpec(memory_space=pltpu.SMEM),
            pl.BlockSpec((None, tq, A_WIDTH), cur),
            pl.BlockSpec((None, tq, 2 * A_KV_WIDTH), cur),
            pl.BlockSpec((None, WINDOW, 2 * A_KV_WIDTH), prev),
            pl.BlockSpec((None, A_KV_WIDTH, tq), cur_t),
            pl.BlockSpec((None, A_KV_WIDTH, WINDOW), prev_t),
            pl.BlockSpec((None, tq, A_WIDTH), cur),
        ],
        out_specs=pl.BlockSpec((None, tq, A_WIDTH), cur),
        out_shape=jax.ShapeDtypeStruct((b, t, A_WIDTH), BF16),
        compiler_params=_params(("parallel", "parallel")),
        name="swa",
    )(sinks, aq, ak2, ak2, avt, avt, ag)


def _diff_kernel(lq_ref, lk_ref, q_ref, k_ref, vt_ref, cg_ref, sg_ref, o_ref, *, lam_init):
    i = pl.program_id(2)
    tq = q_ref.shape[0]
    tk = DIFF_TK
    heads = q_ref.shape[1] // C_HEAD_WIDTH
    e = jnp.exp(jnp.sum(lq_ref[...] * lk_ref[...], axis=-1, keepdims=True))
    lam = e[0:1, :] - e[1:2, :] + lam_init

    lane = lax.broadcasted_iota(jnp.int32, (tq, C_HEAD_WIDTH), 1)
    qs = []
    for h in range(heads):
        q = q_ref[:, h * C_HEAD_WIDTH:(h + 1) * C_HEAD_WIDTH]
        zero = jnp.zeros_like(q)
        qs.append(jnp.concatenate([jnp.where(lane < HEAD_DIM, q, zero), jnp.where(lane >= HEAD_DIM, q, zero)],
                                  axis=0))

    def scores(h, start, size):
        kt = k_ref[pl.ds(start, size), h * C_HEAD_WIDTH:(h + 1) * C_HEAD_WIDTH]
        return lax.dot_general(kt, qs[h], (((1,), (1,)), ((), ())), preferred_element_type=F32)

    def head_step(h, start, size, s, carry, masked):
        m, acc = carry
        vt = vt_ref[h * C_HEAD_WIDTH:(h + 1) * C_HEAD_WIDTH, pl.ds(start, size)]
        vt = jnp.concatenate([vt, jnp.ones((DIFF_SUM_ROWS, size), BF16)], axis=0)
        if masked:
            kpos = lax.broadcasted_iota(jnp.int32, s.shape, 0)
            qpos = lax.broadcasted_iota(jnp.int32, s.shape, 1) & (tq - 1)
            s = jnp.where(kpos <= qpos, s, -jnp.inf)
        m_new = jnp.maximum(m, jnp.max(s, axis=0, keepdims=True))
        alpha = jnp.exp2(m - m_new)
        p = jnp.exp2(s - m_new).astype(BF16)
        acc = alpha * acc + jnp.dot(vt, p, preferred_element_type=F32)
        return m_new, acc

    def step(start, size, carries, masked):
        s_all = [scores(h, start, size) for h in range(heads)]
        return tuple(head_step(h, start, size, s_all[h], carries[h], masked) for h in range(heads))

    init = (jnp.full((1, 2 * tq), -jnp.inf, F32), jnp.zeros((C_HEAD_WIDTH + DIFF_SUM_ROWS, 2 * tq), F32))
    carries = lax.fori_loop(
        0, i // 2, lambda jj, c: step(pl.multiple_of(jj * (2 * tk), 2 * tk), 2 * tk, c, False), (init,) * heads)
    carries = lax.cond(i % 2 == 1, lambda c: step(pl.multiple_of((i - 1) * tk, tk), tk, c, False), lambda c: c,
                       carries)
    carries = step(pl.multiple_of(i * tk, tk), tk, carries, True)
    for h in range(heads):
        _, acc = carries[h]
        o = acc[:C_HEAD_WIDTH, :] * (1.0 / acc[C_HEAD_WIDTH:C_HEAD_WIDTH + 1, :])
        d = o[:, :tq] - lam * o[:, tq:]
        ms = jnp.mean(d * d, axis=0, keepdims=True)
        y = (d * lax.rsqrt(ms + EPS)).T
        y = y * (sg_ref[...] * (1.0 - lam_init))
        sl = slice(h * C_HEAD_WIDTH, (h + 1) * C_HEAD_WIDTH)
        o_ref[:, sl] = (y * cg_ref[:, sl].astype(F32)).astype(BF16)


def _diff(lq, lk, cq, ck, cvt, cg, subln_g, lam_init):
    b, t, _ = cq.shape
    tq = min(DIFF_TQ, t)
    assert tq == DIFF_TK
    width = DIFF_HEADS_PER_STEP * C_HEAD_WIDTH
    qmap = lambda bi, h, i: (bi, i, h)
    kmap = lambda bi, h, i: (bi, 0, h)
    vtmap = lambda bi, h, i: (bi, h, 0)
    const = lambda bi, h, i: (0, 0)
    return pl.pallas_call(
        functools.partial(_diff_kernel, lam_init=lam_init),
        grid=(b, C_HEADS // DIFF_HEADS_PER_STEP, t // tq),
        in_specs=[
            pl.BlockSpec((2, HEAD_DIM), const),
            pl.BlockSpec((2, HEAD_DIM), const),
            pl.BlockSpec((None, tq, width), qmap),
            pl.BlockSpec((None, t, width), kmap),
            pl.BlockSpec((None, width, t), vtmap),
            pl.BlockSpec((None, tq, width), qmap),
            pl.BlockSpec((1, C_HEAD_WIDTH), const),
        ],
        out_specs=pl.BlockSpec((None, tq, width), qmap),
        out_shape=jax.ShapeDtypeStruct((b, t, C_WIDTH), BF16),
        compiler_params=_params(("parallel", "parallel", "arbitrary")),
        name="diff",
    )(lq, lk, cq, ck, cvt, cg, subln_g)


def _merge_kernel(x_ref, ya_ref, yb_ref, yc_ref, mg_ref, wa_ref, wb_ref, wc_ref, wo_ref, o_ref):
    merged = None
    for br, (y_ref, w_ref) in enumerate(((ya_ref, wa_ref), (yb_ref, wb_ref), (yc_ref, wc_ref))):
        proj = jnp.dot(y_ref[...], w_ref[...], preferred_element_type=F32)
        term = mg_ref[:, br * D_MODEL:(br + 1) * D_MODEL].astype(F32) * proj
        merged = term if merged is None else merged + term
    o_ref[...] = x_ref[...] + jnp.dot(merged.astype(BF16), wo_ref[...], preferred_element_type=F32)


def _merge(x2, ya, yb, yc, mg, wa, wb, wc, wo):
    n = x2.shape[0]
    tm = min(ROW_TILE, n)
    row = lambda i: (i, 0)
    const = lambda i: (0, 0)
    return pl.pallas_call(
        _merge_kernel,
        grid=(n // tm,),
        in_specs=[
            pl.BlockSpec((tm, D_MODEL), row),
            pl.BlockSpec((tm, A_WIDTH), row),
            pl.BlockSpec((tm, B_WIDTH), row),
            pl.BlockSpec((tm, C_WIDTH), row),
            pl.BlockSpec((tm, MERGE_COLS), row),
            pl.BlockSpec((A_WIDTH, D_MODEL), const),
            pl.BlockSpec((B_WIDTH, D_MODEL), const),
            pl.BlockSpec((C_WIDTH, D_MODEL), const),
            pl.BlockSpec((D_MODEL, D_MODEL), const),
        ],
        out_specs=pl.BlockSpec((tm, D_MODEL), row),
        out_shape=jax.ShapeDtypeStruct((n, D_MODEL), F32),
        compiler_params=_params(("parallel",)),
        name="merge",
    )(x2, ya, yb, yc, mg, wa, wb, wc, wo)


def _cast_kernel(w_ref, o_ref):
    o_ref[...] = w_ref[...].astype(BF16)


def _cast_bf16(w):
    depth, rows, cols = w.shape
    tr = min(CAST_ROWS, rows)
    spec = pl.BlockSpec((None, tr, cols), lambda l, i: (l, i, 0))
    return pl.pallas_call(
        _cast_kernel,
        grid=(depth, rows // tr),
        in_specs=[spec],
        out_specs=spec,
        out_shape=jax.ShapeDtypeStruct(w.shape, BF16),
        compiler_params=_params(("parallel", "parallel")),
        name="cast",
    )(w)


def kernel(x, norm_g, w_in, attn_q_norm_g, attn_k_norm_g, attn_sinks, w_o_attn, conv_w, conv_b, conv_norm_g, conv_norm_b, w_o_conv, diff_q_norm_g, diff_k_norm_g, lambda_q, lambda_k, diff_subln_g, w_o_diff, w_out):
    b, t, d = x.shape
    n = b * t
    depth = norm_g.shape[0]
    group = lax.broadcasted_iota(jnp.int32, (MXU_COLS, MXU_COLS), 0) // HEAD_DIM
    bd = (group == group.T).astype(BF16)
    tile_gain = lambda g: jnp.tile(g.astype(F32), MXU_COLS // HEAD_DIM).reshape(1, MXU_COLS)
    row = lambda v: v.astype(F32).reshape(1, -1)

    x2 = x.reshape(n, d)
    w_bf = _cast_bf16(w_in)
    for l in range(depth):
        w_k0 = w_bf[l, :, OFF_AK:OFF_AK + HEAD_DIM]
        w_k1 = w_bf[l, :, OFF_AK + HEAD_DIM:OFF_AK + A_KV_WIDTH]
        aq, ak2, avt, ag, yb, cq, ck, cvt, cg, mg = _inproj(
            x2, b, l, row(norm_g[l]), w_bf, jnp.concatenate([w_k0, w_k1, w_k1, w_k0], axis=1),
            w_bf[l, :, OFF_AV:OFF_AV + A_KV_WIDTH].T, w_bf[l, :, OFF_CV:OFF_CV + C_WIDTH].T, bd,
            tile_gain(attn_q_norm_g[l]), tile_gain(attn_k_norm_g[l]),
            tile_gain(diff_q_norm_g[l]), tile_gain(diff_k_norm_g[l]),
            conv_w[l].astype(F32), row(conv_b[l]), row(conv_norm_g[l]), row(conv_norm_b[l]))
        r3 = lambda a: a.reshape(b, t, a.shape[-1])
        ya = _swa(attn_sinks[l].astype(F32), r3(aq), r3(ak2), avt, r3(ag))
        lam_init = 0.8 - 0.6 * math.exp(-0.3 * l)
        yc = _diff(lambda_q[l].astype(F32), lambda_k[l].astype(F32), r3(cq), r3(ck), cvt, r3(cg),
                   row(diff_subln_g[l]), lam_init)
        x2 = _merge(x2, ya.reshape(n, -1), yb, yc.reshape(n, -1), mg,
                    w_o_attn[l].astype(BF16), w_o_conv[l].astype(BF16), w_o_diff[l].astype(BF16),
                    w_out[l].astype(BF16))
    return x2.reshape(b, t, d)
```

```python
import functools
import math

import jax
import jax.numpy as jnp
from jax import lax
from jax.experimental import pallas as pl
from jax.experimental.pallas import tpu as pltpu

F32 = jnp.float32
BF16 = jnp.bfloat16

D_MODEL = 1024
HEAD_DIM = 64
EPS = 1e-6
A_Q_HEADS = 8
A_KV_HEADS = 2
A_GROUP = A_Q_HEADS // A_KV_HEADS
A_WIDTH = A_Q_HEADS * HEAD_DIM
A_KV_WIDTH = A_KV_HEADS * HEAD_DIM
WINDOW = 128
B_WIDTH = D_MODEL // 2
CONV_WIDTH = 31
C_HEADS = 4
C_HEAD_WIDTH = 2 * HEAD_DIM
C_WIDTH = C_HEADS * C_HEAD_WIDTH
N_BRANCH = 3
MERGE_COLS = N_BRANCH * D_MODEL

OFF_AQ = 0
OFF_AK = OFF_AQ + A_WIDTH
OFF_AV = OFF_AK + A_KV_WIDTH
OFF_AG = OFF_AV + A_KV_WIDTH
OFF_BGLU = OFF_AG + A_WIDTH
OFF_BG = OFF_BGLU + 2 * B_WIDTH
OFF_CQ = OFF_BG + B_WIDTH
OFF_CK = OFF_CQ + C_WIDTH
OFF_CV = OFF_CK + C_WIDTH
OFF_CG = OFF_CV + C_WIDTH
OFF_MG = OFF_CG + C_WIDTH
IN_WIDTH = OFF_MG + MERGE_COLS

MXU_COLS = 256
LANES = 128
SUBLANES = 8
VMEM_LIMIT_BYTES = 56 * 1024 * 1024

ROW_TILE = 512
MERGE_TILE = 1024
CAST_ROWS = 256
SWA_TILE = 2048
CONV_HALO = 32
CONV_CHUNK = 64
DIFF_TQ = 512
DIFF_TK = 512
DIFF_HEADS_PER_STEP = 4
DIFF_SUM_ROWS = 16


def _sigmoid(y):
    return 1.0 / (1.0 + jnp.exp(-y))


def _silu(y):
    return y * _sigmoid(y)


def _params(semantics):
    return pltpu.CompilerParams(dimension_semantics=semantics, vmem_limit_bytes=VMEM_LIMIT_BYTES)


def _conv_module(ubuf, gbuf, w_ref, cb_ref, lg_ref, lb_ref, o_ref):
    tt = o_ref.shape[0]
    shifted_rows = tt + CONV_HALO - SUBLANES
    for s in range(1, SUBLANES):
        ubuf[s, 0:shifted_rows, :] = ubuf[0, s:s + shifted_rows, :]
    first = CONV_HALO - (CONV_WIDTH - 1)
    for r0 in range(0, tt, CONV_CHUNK):
        acc = jnp.zeros((CONV_CHUNK, B_WIDTH), F32)
        for k in range(CONV_WIDTH):
            shift = (first + k) % SUBLANES
            base = r0 + first + k - shift
            acc = acc + w_ref[k:k + 1, :] * ubuf[shift, base:base + CONV_CHUNK, :]
        y = acc + cb_ref[...]
        mu = jnp.mean(y, axis=-1, keepdims=True)
        yc = y - mu
        var = jnp.mean(yc * yc, axis=-1, keepdims=True)
        yn = yc * lax.rsqrt(var + EPS) * lg_ref[...] + lb_ref[...]
        o_ref[r0:r0 + CONV_CHUNK, :] = (_silu(yn) * gbuf[r0:r0 + CONV_CHUNK, :]).astype(BF16)


def _inproj_kernel(x_ref, g_ref, w_ref, wak2_ref, wavt_ref, wcvt_ref, bd_ref, aqg_ref, akg_ref, cqg_ref, ckg_ref,
                   cw_ref, cb_ref, lg_ref, lb_ref,
                   aq_o, ak2_o, avt_o, ag_o, yb_o, cq_o, ck_o, cvt_o, cg_o, mg_o, ubuf, gbuf, *, tiles_per_seq):
    tm = x_ref.shape[0]
    x = x_ref[...]
    ms = jnp.mean(x * x, axis=-1, keepdims=True)
    h = (x * lax.rsqrt(ms + EPS) * g_ref[...]).astype(BF16)

    def proj(off, width):
        return jnp.dot(h, w_ref[:, off:off + width], preferred_element_type=F32)

    def head_norm(y, gain, scale):
        w = y.shape[-1]
        sq = (y * y).astype(BF16)
        ssum = jnp.dot(sq, bd_ref[:w, :w], preferred_element_type=F32)
        return y * lax.rsqrt(ssum * (1.0 / HEAD_DIM) + EPS) * (gain * scale)

    qscale_log2 = HEAD_DIM ** -0.5 * math.log2(math.e)
    nt = (((1,), (1,)), ((), ()))
    for c in range(A_WIDTH // MXU_COLS):
        sl = slice(c * MXU_COLS, (c + 1) * MXU_COLS)
        aq_o[:, sl] = head_norm(proj(OFF_AQ + c * MXU_COLS, MXU_COLS), aqg_ref[...], qscale_log2).astype(BF16)
        ag_o[:, sl] = _silu(proj(OFF_AG + c * MXU_COLS, MXU_COLS)).astype(BF16)
    ak2 = jnp.dot(h, wak2_ref[...], preferred_element_type=F32)
    ak2_o[...] = head_norm(ak2, akg_ref[...], 1.0).astype(BF16)
    avt_o[...] = lax.dot_general(wavt_ref[...], h, nt, preferred_element_type=F32).astype(BF16)
    seq_tile = pl.program_id(0) % tiles_per_seq

    @pl.when(seq_tile == 0)
    def _():
        ubuf[0, 0:CONV_HALO, :] = jnp.zeros((CONV_HALO, B_WIDTH), F32)

    @pl.when(seq_tile > 0)
    def _():
        ubuf[0, 0:CONV_HALO, :] = ubuf[0, tm:tm + CONV_HALO, :]

    for c in range(B_WIDTH // MXU_COLS):
        sl = slice(c * MXU_COLS, (c + 1) * MXU_COLS)
        a = proj(OFF_BGLU + c * MXU_COLS, MXU_COLS)
        b = proj(OFF_BGLU + B_WIDTH + c * MXU_COLS, MXU_COLS)
        ubuf[0, CONV_HALO:, sl] = a * _sigmoid(b)
        gbuf[:, sl] = _silu(proj(OFF_BG + c * MXU_COLS, MXU_COLS))
    _conv_module(ubuf, gbuf, cw_ref, cb_ref, lg_ref, lb_ref, yb_o)
    for c in range(C_WIDTH // MXU_COLS):
        sl = slice(c * MXU_COLS, (c + 1) * MXU_COLS)
        cq_o[:, sl] = head_norm(proj(OFF_CQ + c * MXU_COLS, MXU_COLS), cqg_ref[...], qscale_log2).astype(BF16)
        ck_o[:, sl] = head_norm(proj(OFF_CK + c * MXU_COLS, MXU_COLS), ckg_ref[...], 1.0).astype(BF16)
        cg_o[:, sl] = _silu(proj(OFF_CG + c * MXU_COLS, MXU_COLS)).astype(BF16)
    cvt_o[...] = lax.dot_general(wcvt_ref[...], h, nt, preferred_element_type=F32).astype(BF16)
    for c in range(MERGE_COLS // MXU_COLS):
        sl = slice(c * MXU_COLS, (c + 1) * MXU_COLS)
        mg_o[:, sl] = _sigmoid(proj(OFF_MG + c * MXU_COLS, MXU_COLS)).astype(BF16)


def _inproj(x2, batch, layer, norm_g, w_in, w_ak2, w_avt, w_cvt, bd, aqg, akg, cqg, ckg, conv_w, conv_b, ln_g, ln_b):
    n = x2.shape[0]
    t = n // batch
    tm = min(ROW_TILE, t)
    per = t // tm
    const = lambda i: (0, 0)
    row = lambda i: (i, 0)
    rows = lambda w: (pl.BlockSpec((tm, w), row), jax.ShapeDtypeStruct((n, w), BF16))
    cols = lambda w: (pl.BlockSpec((None, w, tm), lambda i: (i // per, 0, i % per)),
                      jax.ShapeDtypeStruct((batch, w, t), BF16))
    outs = (rows(A_WIDTH), rows(2 * A_KV_WIDTH), cols(A_KV_WIDTH), rows(A_WIDTH), rows(B_WIDTH),
            rows(C_WIDTH), rows(C_WIDTH), cols(C_WIDTH), rows(C_WIDTH), rows(MERGE_COLS))
    resident = lambda shape: pl.BlockSpec(shape, const, pipeline_mode=pl.Buffered(1))
    return pl.pallas_call(
        functools.partial(_inproj_kernel, tiles_per_seq=per),
        grid=(n // tm,),
        in_specs=[
            pl.BlockSpec((tm, D_MODEL), row),
            pl.BlockSpec((1, D_MODEL), const),
            pl.BlockSpec((None, D_MODEL, IN_WIDTH), lambda i: (layer, 0, 0), pipeline_mode=pl.Buffered(1)),
            resident((D_MODEL, 2 * A_KV_WIDTH)),
            resident((A_KV_WIDTH, D_MODEL)),
            resident((C_WIDTH, D_MODEL)),
            pl.BlockSpec((MXU_COLS, MXU_COLS), const),
            pl.BlockSpec((1, MXU_COLS), const),
            pl.BlockSpec((1, MXU_COLS), const),
            pl.BlockSpec((1, MXU_COLS), const),
            pl.BlockSpec((1, MXU_COLS), const),
            pl.BlockSpec((CONV_WIDTH, B_WIDTH), const),
            pl.BlockSpec((1, B_WIDTH), const),
            pl.BlockSpec((1, B_WIDTH), const),
            pl.BlockSpec((1, B_WIDTH), const),
        ],
        out_specs=[o[0] for o in outs],
        out_shape=[o[1] for o in outs],
        scratch_shapes=[pltpu.VMEM((SUBLANES, tm + CONV_HALO, B_WIDTH), F32), pltpu.VMEM((tm, B_WIDTH), F32)],
        compiler_params=_params(("arbitrary",)),
        name="inproj",
    )(x2, norm_g, w_in, w_ak2, w_avt, w_cvt, bd, aqg, akg, cqg, ckg, conv_w, conv_b, ln_g, ln_b)


def _swa_kernel(sink_ref, q_ref, kc_ref, kp_ref, vtc_ref, vtp_ref, ag_ref, o_ref):
    i = pl.program_id(1)
    nblk = q_ref.shape[0] // WINDOW
    kfull = jnp.concatenate([kp_ref[...], kc_ref[...]], axis=0)
    vtfull = jnp.concatenate([vtp_ref[...], vtc_ref[...]], axis=1)
    pair = 2 * WINDOW
    c = lax.broadcasted_iota(jnp.int32, (pair, pair), 0)
    t = lax.broadcasted_iota(jnp.int32, (pair, pair), 1) & (WINDOW - 1)
    local = (c > t) & (c <= t + WINDOW)
    first_col = lax.broadcasted_iota(jnp.int32, (1, pair), 1) < WINDOW
    half = lax.broadcasted_iota(jnp.int32, (WINDOW, LANES), 1) < HEAD_DIM
    ones_rows = jnp.ones((DIFF_SUM_ROWS, pair), BF16)
    log2e = math.log2(math.e)
    chains = [(j, kv, par) for j in range(nblk) for kv in range(A_KV_HEADS) for par in range(2)]

    def scores(j, kv, par):
        q = q_ref[j * WINDOW:(j + 1) * WINDOW, :]
        kwin = kfull[j * WINDOW:(j + 2) * WINDOW, :]
        qz = []
        for h in (kv * A_GROUP + par, kv * A_GROUP + par + 2):
            blk = q[:, (h // 2) * LANES:(h // 2 + 1) * LANES]
            keep = half if par == 0 else jnp.logical_not(half)
            qz.append(jnp.where(keep, blk, jnp.zeros_like(blk)))
        qz = jnp.concatenate(qz, axis=0)
        ksel = kwin[:, :LANES] if par == kv else kwin[:, LANES:]
        return lax.dot_general(ksel, qz, (((1,), (1,)), ((), ())), preferred_element_type=F32)

    s_all = [scores(*ch) for ch in chains]
    pieces = {}
    for (j, kv, par), s in zip(chains, s_all):
        heads = (kv * A_GROUP + par, kv * A_GROUP + par + 2)
        mask = local & ((c >= WINDOW) | (i > 0)) if j == 0 else local
        s = jnp.where(mask, s, -jnp.inf)
        sink = jnp.where(first_col, sink_ref[heads[0]] * log2e, sink_ref[heads[1]] * log2e)
        m = jnp.maximum(jnp.max(s, axis=0, keepdims=True), sink)
        p = jnp.exp2(s - m).astype(BF16)
        vtwin = vtfull[kv * HEAD_DIM:(kv + 1) * HEAD_DIM, j * WINDOW:(j + 2) * WINDOW]
        acc = jnp.dot(jnp.concatenate([vtwin, ones_rows], axis=0), p, preferred_element_type=F32)
        denom = acc[HEAD_DIM:HEAD_DIM + 1, :] + jnp.exp2(sink - m)
        o = acc[:HEAD_DIM, :] * (1.0 / denom)
        pieces[j, heads[0]] = o[:, :WINDOW]
        pieces[j, heads[1]] = o[:, WINDOW:]
    for j in range(nblk):
        cols = [jnp.concatenate([pieces[j, 2 * cb], pieces[j, 2 * cb + 1]], axis=0).T for cb in range(A_Q_HEADS // 2)]
        o_all = jnp.concatenate(cols, axis=1)
        gate = ag_ref[j * WINDOW:(j + 1) * WINDOW, :].astype(F32)
        o_ref[j * WINDOW:(j + 1) * WINDOW, :] = (o_all * gate).astype(BF16)


def _swa(sinks, aq, ak2, avt, ag):
    b, t, _ = aq.shape
    tq = min(SWA_TILE, t)
    per = tq // WINDOW
    cur = lambda bi, i: (bi, i, 0)
    prev = lambda bi, i: (bi, jnp.maximum(i * per - 1, 0), 0)
    cur_t = lambda bi, i: (bi, 0, i)
    prev_t = lambda bi, i: (bi, 0, jnp.maximum(i * per - 1, 0))
    return pl.pallas_call(
        _swa_kernel,
        grid=(b, t // tq),
        in_specs=[
            pl.BlockSpec(memory_space=pltpu.SMEM),
            pl.BlockSpec((None, tq, A_WIDTH), cur),
            pl.BlockSpec((None, tq, 2 * A_KV_WIDTH), cur),
            pl.BlockSpec((None, WINDOW, 2 * A_KV_WIDTH), prev),
            pl.BlockSpec((None, A_KV_WIDTH, tq), cur_t),
            pl.BlockSpec((None, A_KV_WIDTH, WINDOW), prev_t),
            pl.BlockSpec((None, tq, A_WIDTH), cur),
        ],
        out_specs=pl.BlockSpec((None, tq, A_WIDTH), cur),
        out_shape=jax.ShapeDtypeStruct((b, t, A_WIDTH), BF16),
        compiler_params=_params(("parallel", "parallel")),
        name="swa",
    )(sinks, aq, ak2, ak2, avt, avt, ag)


def _diff_kernel(lq_ref, lk_ref, q_ref, k_ref, vt_ref, cg_ref, sg_ref, o_ref, *, lam_init):
    i = pl.program_id(2)
    tq = q_ref.shape[0]
    tk = DIFF_TK
    heads = q_ref.shape[1] // C_HEAD_WIDTH
    e = jnp.exp(jnp.sum(lq_ref[...] * lk_ref[...], axis=-1, keepdims=True))
    lam = e[0:1, :] - e[1:2, :] + lam_init

    lane = lax.broadcasted_iota(jnp.int32, (tq, C_HEAD_WIDTH), 1)
    qs = []
    for h in range(heads):
        q = q_ref[:, h * C_HEAD_WIDTH:(h + 1) * C_HEAD_WIDTH]
        zero = jnp.zeros_like(q)
        qs.append(jnp.concatenate([jnp.where(lane < HEAD_DIM, q, zero), jnp.where(lane >= HEAD_DIM, q, zero)],
                                  axis=0))

    def scores(h, start, size):
        kt = k_ref[pl.ds(start, size), h * C_HEAD_WIDTH:(h + 1) * C_HEAD_WIDTH]
        return lax.dot_general(kt, qs[h], (((1,), (1,)), ((), ())), preferred_element_type=F32)

    def head_step(h, start, size, s, carry, masked):
        m, acc = carry
        vt = vt_ref[h * C_HEAD_WIDTH:(h + 1) * C_HEAD_WIDTH, pl.ds(start, size)]
        vt = jnp.concatenate([vt, jnp.ones((DIFF_SUM_ROWS, size), BF16)], axis=0)
        if masked:
            kpos = lax.broadcasted_iota(jnp.int32, s.shape, 0)
            qpos = lax.broadcasted_iota(jnp.int32, s.shape, 1) & (tq - 1)
            s = jnp.where(kpos <= qpos, s, -jnp.inf)
        m_new = jnp.maximum(m, jnp.max(s, axis=0, keepdims=True))
        alpha = jnp.exp2(m - m_new)
        p = jnp.exp2(s - m_new).astype(BF16)
        acc = alpha * acc + jnp.dot(vt, p, preferred_element_type=F32)
        return m_new, acc

    def step(start, size, carries, masked):
        s_all = [scores(h, start, size) for h in range(heads)]
        return tuple(head_step(h, start, size, s_all[h], carries[h], masked) for h in range(heads))

    init = (jnp.full((1, 2 * tq), -jnp.inf, F32), jnp.zeros((C_HEAD_WIDTH + DIFF_SUM_ROWS, 2 * tq), F32))
    carries = lax.fori_loop(
        0, i // 2, lambda jj, c: step(pl.multiple_of(jj * (2 * tk), 2 * tk), 2 * tk, c, False), (init,) * heads)
    carries = lax.cond(i % 2 == 1, lambda c: step(pl.multiple_of((i - 1) * tk, tk), tk, c, False), lambda c: c,
                       carries)
    carries = step(pl.multiple_of(i * tk, tk), tk, carries, True)
    for h in range(heads):
        _, acc = carries[h]
        o = acc[:C_HEAD_WIDTH, :] * (1.0 / acc[C_HEAD_WIDTH:C_HEAD_WIDTH + 1, :])
        d = o[:, :tq] - lam * o[:, tq:]
        ms = jnp.mean(d * d, axis=0, keepdims=True)
        y = (d * lax.rsqrt(ms + EPS)).T
        y = y * (sg_ref[...] * (1.0 - lam_init))
        sl = slice(h * C_HEAD_WIDTH, (h + 1) * C_HEAD_WIDTH)
        o_ref[:, sl] = (y * cg_ref[:, sl].astype(F32)).astype(BF16)


def _diff(lq, lk, cq, ck, cvt, cg, subln_g, lam_init):
    b, t, _ = cq.shape
    tq = min(DIFF_TQ, t)
    assert tq == DIFF_TK
    width = DIFF_HEADS_PER_STEP * C_HEAD_WIDTH
    qmap = lambda bi, h, i: (bi, i, h)
    kmap = lambda bi, h, i: (bi, 0, h)
    vtmap = lambda bi, h, i: (bi, h, 0)
    const = lambda bi, h, i: (0, 0)
    return pl.pallas_call(
        functools.partial(_diff_kernel, lam_init=lam_init),
        grid=(b, C_HEADS // DIFF_HEADS_PER_STEP, t // tq),
        in_specs=[
            pl.BlockSpec((2, HEAD_DIM), const),
            pl.BlockSpec((2, HEAD_DIM), const),
            pl.BlockSpec((None, tq, width), qmap),
            pl.BlockSpec((None, t, width), kmap),
            pl.BlockSpec((None, width, t), vtmap),
            pl.BlockSpec((None, tq, width), qmap),
            pl.BlockSpec((1, C_HEAD_WIDTH), const),
        ],
        out_specs=pl.BlockSpec((None, tq, width), qmap),
        out_shape=jax.ShapeDtypeStruct((b, t, C_WIDTH), BF16),
        compiler_params=_params(("parallel", "parallel", "arbitrary")),
        name="diff",
    )(lq, lk, cq, ck, cvt, cg, subln_g)


def _merge_kernel(x_ref, ya_ref, yb_ref, yc_ref, mg_ref, wa_ref, wb_ref, wc_ref, wo_ref, o_ref):
    merged = None
    for br, (y_ref, w_ref) in enumerate(((ya_ref, wa_ref), (yb_ref, wb_ref), (yc_ref, wc_ref))):
        proj = jnp.dot(y_ref[...], w_ref[...], preferred_element_type=F32)
        term = mg_ref[:, br * D_MODEL:(br + 1) * D_MODEL].astype(F32) * proj
        merged = term if merged is None else merged + term
    o_ref[...] = x_ref[...] + jnp.dot(merged.astype(BF16), wo_ref[...], preferred_element_type=F32)


def _merge(x2, ya, yb, yc, mg, wa, wb, wc, wo):
    n = x2.shape[0]
    tm = min(MERGE_TILE, n)
    row = lambda i: (i, 0)
    const = lambda i: (0, 0)
    return pl.pallas_call(
        _merge_kernel,
        grid=(n // tm,),
        in_specs=[
            pl.BlockSpec((tm, D_MODEL), row),
            pl.BlockSpec((tm, A_WIDTH), row),
            pl.BlockSpec((tm, B_WIDTH), row),
            pl.BlockSpec((tm, C_WIDTH), row),
            pl.BlockSpec((tm, MERGE_COLS), row),
            pl.BlockSpec((A_WIDTH, D_MODEL), const),
            pl.BlockSpec((B_WIDTH, D_MODEL), const),
            pl.BlockSpec((C_WIDTH, D_MODEL), const),
            pl.BlockSpec((D_MODEL, D_MODEL), const),
        ],
        out_specs=pl.BlockSpec((tm, D_MODEL), row),
        out_shape=jax.ShapeDtypeStruct((n, D_MODEL), F32),
        compiler_params=_params(("parallel",)),
        name="merge",
    )(x2, ya, yb, yc, mg, wa, wb, wc, wo)


def _cast_kernel(w_ref, o_ref):
    o_ref[...] = w_ref[...].astype(BF16)


def _cast_bf16(w):
    depth, rows, cols = w.shape
    tr = min(CAST_ROWS, rows)
    spec = pl.BlockSpec((None, tr, cols), lambda l, i: (l, i, 0))
    return pl.pallas_call(
        _cast_kernel,
        grid=(depth, rows // tr),
        in_specs=[spec],
        out_specs=spec,
        out_shape=jax.ShapeDtypeStruct(w.shape, BF16),
        compiler_params=_params(("parallel", "parallel")),
        name="cast",
    )(w)


def kernel(x, norm_g, w_in, attn_q_norm_g, attn_k_norm_g, attn_sinks, w_o_attn, conv_w, conv_b, conv_norm_g, conv_norm_b, w_o_conv, diff_q_norm_g, diff_k_norm_g, lambda_q, lambda_k, diff_subln_g, w_o_diff, w_out):
    b, t, d = x.shape
    n = b * t
    depth = norm_g.shape[0]
    group = lax.broadcasted_iota(jnp.int32, (MXU_COLS, MXU_COLS), 0) // HEAD_DIM
    bd = (group == group.T).astype(BF16)
    tile_gain = lambda g: jnp.tile(g.astype(F32), MXU_COLS // HEAD_DIM).reshape(1, MXU_COLS)
    row = lambda v: v.astype(F32).reshape(1, -1)

    x2 = x.reshape(n, d)
    w_bf = _cast_bf16(w_in)
    for l in range(depth):
        w_k0 = w_bf[l, :, OFF_AK:OFF_AK + HEAD_DIM]
        w_k1 = w_bf[l, :, OFF_AK + HEAD_DIM:OFF_AK + A_KV_WIDTH]
        aq, ak2, avt, ag, yb, cq, ck, cvt, cg, mg = _inproj(
            x2, b, l, row(norm_g[l]), w_bf, jnp.concatenate([w_k0, w_k1, w_k1, w_k0], axis=1),
            w_bf[l, :, OFF_AV:OFF_AV + A_KV_WIDTH].T, w_bf[l, :, OFF_CV:OFF_CV + C_WIDTH].T, bd,
            tile_gain(attn_q_norm_g[l]), tile_gain(attn_k_norm_g[l]),
            tile_gain(diff_q_norm_g[l]), tile_gain(diff_k_norm_g[l]),
            conv_w[l].astype(F32), row(conv_b[l]), row(conv_norm_g[l]), row(conv_norm_b[l]))
        r3 = lambda a: a.reshape(b, t, a.shape[-1])
        ya = _swa(attn_sinks[l].astype(F32), r3(aq), r3(ak2), avt, r3(ag))
        lam_init = 0.8 - 0.6 * math.exp(-0.3 * l)
        yc = _diff(lambda_q[l].astype(F32), lambda_k[l].astype(F32), r3(cq), r3(ck), cvt, r3(cg),
                   row(diff_subln_g[l]), lam_init)
        x2 = _merge(x2, ya.reshape(n, -1), yb, yc.reshape(n, -1), mg,
                    w_o_attn[l].astype(BF16), w_o_conv[l].astype(BF16), w_o_diff[l].astype(BF16),
                    w_out[l].astype(BF16))
    return x2.reshape(b, t, d)
```

```python
import functools
import math

import jax
import jax.numpy as jnp
from jax import lax
from jax.experimental import pallas as pl
from jax.experimental.pallas import tpu as pltpu

F32 = jnp.float32
BF16 = jnp.bfloat16

D_MODEL = 1024
HEAD_DIM = 64
EPS = 1e-6
A_Q_HEADS = 8
A_KV_HEADS = 2
A_GROUP = A_Q_HEADS // A_KV_HEADS
A_WIDTH = A_Q_HEADS * HEAD_DIM
A_KV_WIDTH = A_KV_HEADS * HEAD_DIM
WINDOW = 128
B_WIDTH = D_MODEL // 2
CONV_WIDTH = 31
C_HEADS = 4
C_HEAD_WIDTH = 2 * HEAD_DIM
C_WIDTH = C_HEADS * C_HEAD_WIDTH
N_BRANCH = 3
MERGE_COLS = N_BRANCH * D_MODEL

OFF_AQ = 0
OFF_AK = OFF_AQ + A_WIDTH
OFF_AV = OFF_AK + A_KV_WIDTH
OFF_AG = OFF_AV + A_KV_WIDTH
OFF_BGLU = OFF_AG + A_WIDTH
OFF_BG = OFF_BGLU + 2 * B_WIDTH
OFF_CQ = OFF_BG + B_WIDTH
OFF_CK = OFF_CQ + C_WIDTH
OFF_CV = OFF_CK + C_WIDTH
OFF_CG = OFF_CV + C_WIDTH
OFF_MG = OFF_CG + C_WIDTH
IN_WIDTH = OFF_MG + MERGE_COLS

MXU_COLS = 256
LANES = 128
SUBLANES = 8
VMEM_LIMIT_BYTES = 56 * 1024 * 1024

ROW_TILE = 512
MERGE_TILE = 1024
CAST_ROWS = 256
SWA_TILE = 2048
CONV_HALO = 32
CONV_CHUNK = 64
DIFF_TQ = 512
DIFF_TK = 512
DIFF_HEADS_PER_STEP = 4
DIFF_SUM_ROWS = 16


def _sigmoid(y):
    return 1.0 / (1.0 + jnp.exp(-y))


def _silu(y):
    return y * _sigmoid(y)


def _params(semantics):
    return pltpu.CompilerParams(dimension_semantics=semantics, vmem_limit_bytes=VMEM_LIMIT_BYTES)


def _conv_module(ubuf, gbuf, w_ref, cb_ref, lg_ref, lb_ref, o_ref):
    tt = o_ref.shape[0]
    shifted_rows = tt + CONV_HALO - SUBLANES
    for s in range(1, SUBLANES):
        ubuf[s, 0:shifted_rows, :] = ubuf[0, s:s + shifted_rows, :]
    first = CONV_HALO - (CONV_WIDTH - 1)
    for r0 in range(0, tt, CONV_CHUNK):
        acc = jnp.zeros((CONV_CHUNK, B_WIDTH), F32)
        for k in range(CONV_WIDTH):
            shift = (first + k) % SUBLANES
            base = r0 + first + k - shift
            acc = acc + w_ref[k:k + 1, :] * ubuf[shift, base:base + CONV_CHUNK, :]
        y = acc + cb_ref[...]
        mu = jnp.mean(y, axis=-1, keepdims=True)
        yc = y - mu
        var = jnp.mean(yc * yc, axis=-1, keepdims=True)
        yn = yc * lax.rsqrt(var + EPS) * lg_ref[...] + lb_ref[...]
        o_ref[r0:r0 + CONV_CHUNK, :] = (_silu(yn) * gbuf[r0:r0 + CONV_CHUNK, :]).astype(BF16)


def _inproj_kernel(x_ref, g_ref, w_ref, wak2_ref, wavt_ref, wcvt_ref, bd_ref, aqg_ref, akg_ref, cqg_ref, ckg_ref,
                   cw_ref, cb_ref, lg_ref, lb_ref,
                   aq_o, ak2_o, avt_o, ag_o, yb_o, cq_o, ck_o, cvt_o, cg_o, ubuf, gbuf, *, tiles_per_seq):
    tm = x_ref.shape[0]
    x = x_ref[...]
    ms = jnp.mean(x * x, axis=-1, keepdims=True)
    h = (x * lax.rsqrt(ms + EPS) * g_ref[...]).astype(BF16)

    def proj(off, width):
        return jnp.dot(h, w_ref[:, off:off + width], preferred_element_type=F32)

    def head_norm(y, gain, scale):
        w = y.shape[-1]
        sq = (y * y).astype(BF16)
        ssum = jnp.dot(sq, bd_ref[:w, :w], preferred_element_type=F32)
        return y * lax.rsqrt(ssum * (1.0 / HEAD_DIM) + EPS) * (gain * scale)

    qscale_log2 = HEAD_DIM ** -0.5 * math.log2(math.e)
    nt = (((1,), (1,)), ((), ()))
    for c in range(A_WIDTH // MXU_COLS):
        sl = slice(c * MXU_COLS, (c + 1) * MXU_COLS)
        aq_o[:, sl] = head_norm(proj(OFF_AQ + c * MXU_COLS, MXU_COLS), aqg_ref[...], qscale_log2).astype(BF16)
        ag_o[:, sl] = _silu(proj(OFF_AG + c * MXU_COLS, MXU_COLS)).astype(BF16)
    ak2 = jnp.dot(h, wak2_ref[...], preferred_element_type=F32)
    ak2_o[...] = head_norm(ak2, akg_ref[...], 1.0).astype(BF16)
    avt_o[...] = lax.dot_general(wavt_ref[...], h, nt, preferred_element_type=F32).astype(BF16)
    seq_tile = pl.program_id(0) % tiles_per_seq

    @pl.when(seq_tile == 0)
    def _():
        ubuf[0, 0:CONV_HALO, :] = jnp.zeros((CONV_HALO, B_WIDTH), F32)

    @pl.when(seq_tile > 0)
    def _():
        ubuf[0, 0:CONV_HALO, :] = ubuf[0, tm:tm + CONV_HALO, :]

    for c in range(B_WIDTH // MXU_COLS):
        sl = slice(c * MXU_COLS, (c + 1) * MXU_COLS)
        a = proj(OFF_BGLU + c * MXU_COLS, MXU_COLS)
        b = proj(OFF_BGLU + B_WIDTH + c * MXU_COLS, MXU_COLS)
        ubuf[0, CONV_HALO:, sl] = a * _sigmoid(b)
        gbuf[:, sl] = _silu(proj(OFF_BG + c * MXU_COLS, MXU_COLS))
    _conv_module(ubuf, gbuf, cw_ref, cb_ref, lg_ref, lb_ref, yb_o)
    for c in range(C_WIDTH // MXU_COLS):
        sl = slice(c * MXU_COLS, (c + 1) * MXU_COLS)
        cq_o[:, sl] = head_norm(proj(OFF_CQ + c * MXU_COLS, MXU_COLS), cqg_ref[...], qscale_log2).astype(BF16)
        ck_o[:, sl] = head_norm(proj(OFF_CK + c * MXU_COLS, MXU_COLS), ckg_ref[...], 1.0).astype(BF16)
        cg_o[:, sl] = _silu(proj(OFF_CG + c * MXU_COLS, MXU_COLS)).astype(BF16)
    cvt_o[...] = lax.dot_general(wcvt_ref[...], h, nt, preferred_element_type=F32).astype(BF16)


def _inproj(x2, batch, layer, norm_g, w_in, w_ak2, w_avt, w_cvt, bd, aqg, akg, cqg, ckg, conv_w, conv_b, ln_g, ln_b):
    n = x2.shape[0]
    t = n // batch
    tm = min(ROW_TILE, t)
    per = t // tm
    const = lambda i: (0, 0)
    row = lambda i: (i, 0)
    rows = lambda w: (pl.BlockSpec((tm, w), row), jax.ShapeDtypeStruct((n, w), BF16))
    cols = lambda w: (pl.BlockSpec((None, w, tm), lambda i: (i // per, 0, i % per)),
                      jax.ShapeDtypeStruct((batch, w, t), BF16))
    outs = (rows(A_WIDTH), rows(2 * A_KV_WIDTH), cols(A_KV_WIDTH), rows(A_WIDTH), rows(B_WIDTH),
            rows(C_WIDTH), rows(C_WIDTH), cols(C_WIDTH), rows(C_WIDTH))
    resident = lambda shape: pl.BlockSpec(shape, const, pipeline_mode=pl.Buffered(1))
    return pl.pallas_call(
        functools.partial(_inproj_kernel, tiles_per_seq=per),
        grid=(n // tm,),
        in_specs=[
            pl.BlockSpec((tm, D_MODEL), row),
            pl.BlockSpec((1, D_MODEL), const),
            pl.BlockSpec((None, D_MODEL, OFF_MG), lambda i: (layer, 0, 0), pipeline_mode=pl.Buffered(1)),
            resident((D_MODEL, 2 * A_KV_WIDTH)),
            resident((A_KV_WIDTH, D_MODEL)),
            resident((C_WIDTH, D_MODEL)),
            pl.BlockSpec((MXU_COLS, MXU_COLS), const),
            pl.BlockSpec((1, MXU_COLS), const),
            pl.BlockSpec((1, MXU_COLS), const),
            pl.BlockSpec((1, MXU_COLS), const),
            pl.BlockSpec((1, MXU_COLS), const),
            pl.BlockSpec((CONV_WIDTH, B_WIDTH), const),
            pl.BlockSpec((1, B_WIDTH), const),
            pl.BlockSpec((1, B_WIDTH), const),
            pl.BlockSpec((1, B_WIDTH), const),
        ],
        out_specs=[o[0] for o in outs],
        out_shape=[o[1] for o in outs],
        scratch_shapes=[pltpu.VMEM((SUBLANES, tm + CONV_HALO, B_WIDTH), F32), pltpu.VMEM((tm, B_WIDTH), F32)],
        compiler_params=_params(("arbitrary",)),
        name="inproj",
    )(x2, norm_g, w_in, w_ak2, w_avt, w_cvt, bd, aqg, akg, cqg, ckg, conv_w, conv_b, ln_g, ln_b)


def _swa_kernel(sink_ref, q_ref, kc_ref, kp_ref, vtc_ref, vtp_ref, ag_ref, o_ref):
    i = pl.program_id(1)
    nblk = q_ref.shape[0] // WINDOW
    kfull = jnp.concatenate([kp_ref[...], kc_ref[...]], axis=0)
    vtfull = jnp.concatenate([vtp_ref[...], vtc_ref[...]], axis=1)
    pair = 2 * WINDOW
    c = lax.broadcasted_iota(jnp.int32, (pair, pair), 0)
    t = lax.broadcasted_iota(jnp.int32, (pair, pair), 1) & (WINDOW - 1)
    local = (c > t) & (c <= t + WINDOW)
    first_col = lax.broadcasted_iota(jnp.int32, (1, pair), 1) < WINDOW
    half = lax.broadcasted_iota(jnp.int32, (WINDOW, LANES), 1) < HEAD_DIM
    ones_rows = jnp.ones((DIFF_SUM_ROWS, pair), BF16)
    log2e = math.log2(math.e)
    chains = [(j, kv, par) for j in range(nblk) for kv in range(A_KV_HEADS) for par in range(2)]

    def scores(j, kv, par):
        q = q_ref[j * WINDOW:(j + 1) * WINDOW, :]
        kwin = kfull[j * WINDOW:(j + 2) * WINDOW, :]
        qz = []
        for h in (kv * A_GROUP + par, kv * A_GROUP + par + 2):
            blk = q[:, (h // 2) * LANES:(h // 2 + 1) * LANES]
            keep = half if par == 0 else jnp.logical_not(half)
            qz.append(jnp.where(keep, blk, jnp.zeros_like(blk)))
        qz = jnp.concatenate(qz, axis=0)
        ksel = kwin[:, :LANES] if par == kv else kwin[:, LANES:]
        return lax.dot_general(ksel, qz, (((1,), (1,)), ((), ())), preferred_element_type=F32)

    s_all = [scores(*ch) for ch in chains]
    pieces = {}
    for (j, kv, par), s in zip(chains, s_all):
        heads = (kv * A_GROUP + par, kv * A_GROUP + par + 2)
        mask = local & ((c >= WINDOW) | (i > 0)) if j == 0 else local
        s = jnp.where(mask, s, -jnp.inf)
        sink = jnp.where(first_col, sink_ref[heads[0]] * log2e, sink_ref[heads[1]] * log2e)
        m = jnp.maximum(jnp.max(s, axis=0, keepdims=True), sink)
        p = jnp.exp2(s - m).astype(BF16)
        vtwin = vtfull[kv * HEAD_DIM:(kv + 1) * HEAD_DIM, j * WINDOW:(j + 2) * WINDOW]
        acc = jnp.dot(jnp.concatenate([vtwin, ones_rows], axis=0), p, preferred_element_type=F32)
        denom = acc[HEAD_DIM:HEAD_DIM + 1, :] + jnp.exp2(sink - m)
        o = acc[:HEAD_DIM, :] * (1.0 / denom)
        pieces[j, heads[0]] = o[:, :WINDOW]
        pieces[j, heads[1]] = o[:, WINDOW:]
    for j in range(nblk):
        cols = [jnp.concatenate([pieces[j, 2 * cb], pieces[j, 2 * cb + 1]], axis=0).T for cb in range(A_Q_HEADS // 2)]
        o_all = jnp.concatenate(cols, axis=1)
        gate = ag_ref[j * WINDOW:(j + 1) * WINDOW, :].astype(F32)
        o_ref[j * WINDOW:(j + 1) * WINDOW, :] = (o_all * gate).astype(BF16)


def _swa(sinks, aq, ak2, avt, ag):
    b, t, _ = aq.shape
    tq = min(SWA_TILE, t)
    per = tq // WINDOW
    cur = lambda bi, i: (bi, i, 0)
    prev = lambda bi, i: (bi, jnp.maximum(i * per - 1, 0), 0)
    cur_t = lambda bi, i: (bi, 0, i)
    prev_t = lambda bi, i: (bi, 0, jnp.maximum(i * per - 1, 0))
    return pl.pallas_call(
        _swa_kernel,
        grid=(b, t // tq),
        in_specs=[
            pl.BlockSpec(memory_space=pltpu.SMEM),
            pl.BlockSpec((None, tq, A_WIDTH), cur),
            pl.BlockSpec((None, tq, 2 * A_KV_WIDTH), cur),
            pl.BlockSpec((None, WINDOW, 2 * A_KV_WIDTH), prev),
            pl.BlockSpec((None, A_KV_WIDTH, tq), cur_t),
            pl.BlockSpec((None, A_KV_WIDTH, WINDOW), prev_t),
            pl.BlockSpec((None, tq, A_WIDTH), cur),
        ],
        out_specs=pl.BlockSpec((None, tq, A_WIDTH), cur),
        out_shape=jax.ShapeDtypeStruct((b, t, A_WIDTH), BF16),
        compiler_params=_params(("parallel", "parallel")),
        name="swa",
    )(sinks, aq, ak2, ak2, avt, avt, ag)


def _diff_kernel(lq_ref, lk_ref, q_ref, k_ref, vt_ref, cg_ref, sg_ref, o_ref, *, lam_init):
    i = pl.program_id(2)
    tq = q_ref.shape[0]
    tk = DIFF_TK
    heads = q_ref.shape[1] // C_HEAD_WIDTH
    e = jnp.exp(jnp.sum(lq_ref[...] * lk_ref[...], axis=-1, keepdims=True))
    lam = e[0:1, :] - e[1:2, :] + lam_init

    lane = lax.broadcasted_iota(jnp.int32, (tq, C_HEAD_WIDTH), 1)
    qs = []
    for h in range(heads):
        q = q_ref[:, h * C_HEAD_WIDTH:(h + 1) * C_HEAD_WIDTH]
        zero = jnp.zeros_like(q)
        qs.append(jnp.concatenate([jnp.where(lane < HEAD_DIM, q, zero), jnp.where(lane >= HEAD_DIM, q, zero)],
                                  axis=0))

    def scores(h, start, size):
        kt = k_ref[pl.ds(start, size), h * C_HEAD_WIDTH:(h + 1) * C_HEAD_WIDTH]
        return lax.dot_general(kt, qs[h], (((1,), (1,)), ((), ())), preferred_element_type=F32)

    def head_step(h, start, size, s, carry, masked):
        m, acc = carry
        vt = vt_ref[h * C_HEAD_WIDTH:(h + 1) * C_HEAD_WIDTH, pl.ds(start, size)]
        vt = jnp.concatenate([vt, jnp.ones((DIFF_SUM_ROWS, size), BF16)], axis=0)
        if masked:
            kpos = lax.broadcasted_iota(jnp.int32, s.shape, 0)
            qpos = lax.broadcasted_iota(jnp.int32, s.shape, 1) & (tq - 1)
            s = jnp.where(kpos <= qpos, s, -jnp.inf)
        m_new = jnp.maximum(m, jnp.max(s, axis=0, keepdims=True))
        alpha = jnp.exp2(m - m_new)
        p = jnp.exp2(s - m_new).astype(BF16)
        acc = alpha * acc + jnp.dot(vt, p, preferred_element_type=F32)
        return m_new, acc

    def step(start, size, carries, masked):
        s_all = [scores(h, start, size) for h in range(heads)]
        return tuple(head_step(h, start, size, s_all[h], carries[h], masked) for h in range(heads))

    init = (jnp.full((1, 2 * tq), -jnp.inf, F32), jnp.zeros((C_HEAD_WIDTH + DIFF_SUM_ROWS, 2 * tq), F32))
    carries = lax.fori_loop(
        0, i // 2, lambda jj, c: step(pl.multiple_of(jj * (2 * tk), 2 * tk), 2 * tk, c, False), (init,) * heads)
    carries = lax.cond(i % 2 == 1, lambda c: step(pl.multiple_of((i - 1) * tk, tk), tk, c, False), lambda c: c,
                       carries)
    carries = step(pl.multiple_of(i * tk, tk), tk, carries, True)
    for h in range(heads):
        _, acc = carries[h]
        o = acc[:C_HEAD_WIDTH, :] * (1.0 / acc[C_HEAD_WIDTH:C_HEAD_WIDTH + 1, :])
        d = o[:, :tq] - lam * o[:, tq:]
        ms = jnp.mean(d * d, axis=0, keepdims=True)
        y = (d * lax.rsqrt(ms + EPS)).T
        y = y * (sg_ref[...] * (1.0 - lam_init))
        sl = slice(h * C_HEAD_WIDTH, (h + 1) * C_HEAD_WIDTH)
        o_ref[:, sl] = (y * cg_ref[:, sl].astype(F32)).astype(BF16)


def _diff(lq, lk, cq, ck, cvt, cg, subln_g, lam_init):
    b, t, _ = cq.shape
    tq = min(DIFF_TQ, t)
    assert tq == DIFF_TK
    width = DIFF_HEADS_PER_STEP * C_HEAD_WIDTH
    qmap = lambda bi, h, i: (bi, i, h)
    kmap = lambda bi, h, i: (bi, 0, h)
    vtmap = lambda bi, h, i: (bi, h, 0)
    const = lambda bi, h, i: (0, 0)
    return pl.pallas_call(
        functools.partial(_diff_kernel, lam_init=lam_init),
        grid=(b, C_HEADS // DIFF_HEADS_PER_STEP, t // tq),
        in_specs=[
            pl.BlockSpec((2, HEAD_DIM), const),
            pl.BlockSpec((2, HEAD_DIM), const),
            pl.BlockSpec((None, tq, width), qmap),
            pl.BlockSpec((None, t, width), kmap),
            pl.BlockSpec((None, width, t), vtmap),
            pl.BlockSpec((None, tq, width), qmap),
            pl.BlockSpec((1, C_HEAD_WIDTH), const),
        ],
        out_specs=pl.BlockSpec((None, tq, width), qmap),
        out_shape=jax.ShapeDtypeStruct((b, t, C_WIDTH), BF16),
        compiler_params=_params(("parallel", "parallel", "arbitrary")),
        name="diff",
    )(lq, lk, cq, ck, cvt, cg, subln_g)


def _merge_kernel(x_ref, g_ref, ya_ref, yb_ref, yc_ref, wg_ref, wa_ref, wb_ref, wc_ref, wo_ref, o_ref, merged):
    x = x_ref[...]
    ms = jnp.mean(x * x, axis=-1, keepdims=True)
    h = (x * lax.rsqrt(ms + EPS) * g_ref[...]).astype(BF16)
    branches = ((ya_ref, wa_ref), (yb_ref, wb_ref), (yc_ref, wc_ref))
    for c in range(D_MODEL // MXU_COLS):
        sl = slice(c * MXU_COLS, (c + 1) * MXU_COLS)
        total = None
        for br, (y_ref, w_ref) in enumerate(branches):
            gate_cols = slice(br * D_MODEL + c * MXU_COLS, br * D_MODEL + (c + 1) * MXU_COLS)
            gate = _sigmoid(jnp.dot(h, wg_ref[:, gate_cols], preferred_element_type=F32))
            term = gate * jnp.dot(y_ref[...], w_ref[:, sl], preferred_element_type=F32)
            total = term if total is None else total + term
        merged[:, sl] = total.astype(BF16)
    o_ref[...] = x + jnp.dot(merged[...], wo_ref[...], preferred_element_type=F32)


def _merge(x2, norm_g, ya, yb, yc, wg, wa, wb, wc, wo):
    n = x2.shape[0]
    tm = min(MERGE_TILE, n)
    row = lambda i: (i, 0)
    const = lambda i: (0, 0)
    resident = lambda shape: pl.BlockSpec(shape, const, pipeline_mode=pl.Buffered(1))
    return pl.pallas_call(
        _merge_kernel,
        grid=(n // tm,),
        in_specs=[
            pl.BlockSpec((tm, D_MODEL), row),
            pl.BlockSpec((1, D_MODEL), const),
            pl.BlockSpec((tm, A_WIDTH), row),
            pl.BlockSpec((tm, B_WIDTH), row),
            pl.BlockSpec((tm, C_WIDTH), row),
            resident((D_MODEL, MERGE_COLS)),
            resident((A_WIDTH, D_MODEL)),
            resident((B_WIDTH, D_MODEL)),
            resident((C_WIDTH, D_MODEL)),
            resident((D_MODEL, D_MODEL)),
        ],
        out_specs=pl.BlockSpec((tm, D_MODEL), row),
        out_shape=jax.ShapeDtypeStruct((n, D_MODEL), F32),
        scratch_shapes=[pltpu.VMEM((tm, D_MODEL), BF16)],
        compiler_params=_params(("parallel",)),
        name="merge",
    )(x2, norm_g, ya, yb, yc, wg, wa, wb, wc, wo)


def _cast_kernel(w_ref, o_ref):
    o_ref[...] = w_ref[...].astype(BF16)


def _cast_bf16(w):
    depth, rows, cols = w.shape
    tr = min(CAST_ROWS, rows)
    spec = pl.BlockSpec((None, tr, cols), lambda l, i: (l, i, 0))
    return pl.pallas_call(
        _cast_kernel,
        grid=(depth, rows // tr),
        in_specs=[spec],
        out_specs=spec,
        out_shape=jax.ShapeDtypeStruct(w.shape, BF16),
        compiler_params=_params(("parallel", "parallel")),
        name="cast",
    )(w)


def kernel(x, norm_g, w_in, attn_q_norm_g, attn_k_norm_g, attn_sinks, w_o_attn, conv_w, conv_b, conv_norm_g, conv_norm_b, w_o_conv, diff_q_norm_g, diff_k_norm_g, lambda_q, lambda_k, diff_subln_g, w_o_diff, w_out):
    b, t, d = x.shape
    n = b * t
    depth = norm_g.shape[0]
    group = lax.broadcasted_iota(jnp.int32, (MXU_COLS, MXU_COLS), 0) // HEAD_DIM
    bd = (group == group.T).astype(BF16)
    tile_gain = lambda g: jnp.tile(g.astype(F32), MXU_COLS // HEAD_DIM).reshape(1, MXU_COLS)
    row = lambda v: v.astype(F32).reshape(1, -1)

    x2 = x.reshape(n, d)
    w_bf = _cast_bf16(w_in)
    for l in range(depth):
        w_k0 = w_bf[l, :, OFF_AK:OFF_AK + HEAD_DIM]
        w_k1 = w_bf[l, :, OFF_AK + HEAD_DIM:OFF_AK + A_KV_WIDTH]
        aq, ak2, avt, ag, yb, cq, ck, cvt, cg = _inproj(
            x2, b, l, row(norm_g[l]), w_bf, jnp.concatenate([w_k0, w_k1, w_k1, w_k0], axis=1),
            w_bf[l, :, OFF_AV:OFF_AV + A_KV_WIDTH].T, w_bf[l, :, OFF_CV:OFF_CV + C_WIDTH].T, bd,
            tile_gain(attn_q_norm_g[l]), tile_gain(attn_k_norm_g[l]),
            tile_gain(diff_q_norm_g[l]), tile_gain(diff_k_norm_g[l]),
            conv_w[l].astype(F32), row(conv_b[l]), row(conv_norm_g[l]), row(conv_norm_b[l]))
        r3 = lambda a: a.reshape(b, t, a.shape[-1])
        ya = _swa(attn_sinks[l].astype(F32), r3(aq), r3(ak2), avt, r3(ag))
        lam_init = 0.8 - 0.6 * math.exp(-0.3 * l)
        yc = _diff(lambda_q[l].astype(F32), lambda_k[l].astype(F32), r3(cq), r3(ck), cvt, r3(cg),
                   row(diff_subln_g[l]), lam_init)
        x2 = _merge(x2, row(norm_g[l]), ya.reshape(n, -1), yb, yc.reshape(n, -1), w_bf[l, :, OFF_MG:],
                    w_o_attn[l].astype(BF16), w_o_conv[l].astype(BF16), w_o_diff[l].astype(BF16),
                    w_out[l].astype(BF16))
    return x2.reshape(b, t, d)
```

```python
import functools
import math

import jax
import jax.numpy as jnp
from jax import lax
from jax.experimental import pallas as pl
from jax.experimental.pallas import tpu as pltpu

F32 = jnp.float32
BF16 = jnp.bfloat16

D_MODEL = 1024
HEAD_DIM = 64
EPS = 1e-6
A_Q_HEADS = 8
A_KV_HEADS = 2
A_GROUP = A_Q_HEADS // A_KV_HEADS
A_WIDTH = A_Q_HEADS * HEAD_DIM
A_KV_WIDTH = A_KV_HEADS * HEAD_DIM
WINDOW = 128
B_WIDTH = D_MODEL // 2
CONV_WIDTH = 31
C_HEADS = 4
C_HEAD_WIDTH = 2 * HEAD_DIM
C_WIDTH = C_HEADS * C_HEAD_WIDTH
N_BRANCH = 3
MERGE_COLS = N_BRANCH * D_MODEL

OFF_AQ = 0
OFF_AK = OFF_AQ + A_WIDTH
OFF_AV = OFF_AK + A_KV_WIDTH
OFF_AG = OFF_AV + A_KV_WIDTH
OFF_BGLU = OFF_AG + A_WIDTH
OFF_BG = OFF_BGLU + 2 * B_WIDTH
OFF_CQ = OFF_BG + B_WIDTH
OFF_CK = OFF_CQ + C_WIDTH
OFF_CV = OFF_CK + C_WIDTH
OFF_CG = OFF_CV + C_WIDTH
OFF_MG = OFF_CG + C_WIDTH
IN_WIDTH = OFF_MG + MERGE_COLS

MXU_COLS = 256
LANES = 128
SUBLANES = 8
VMEM_LIMIT_BYTES = 56 * 1024 * 1024

ROW_TILE = 512
MERGE_TILE = 1024
CAST_ROWS = 256
SWA_TILE = 2048
CONV_HALO = 32
CONV_CHUNK = 64
DIFF_TQ = 512
DIFF_TK = 512
DIFF_HEADS_PER_STEP = 4
DIFF_SUM_ROWS = 16


def _sigmoid(y):
    return 1.0 / (1.0 + jnp.exp(-y))


def _silu(y):
    return y * _sigmoid(y)


def _params(semantics):
    return pltpu.CompilerParams(dimension_semantics=semantics, vmem_limit_bytes=VMEM_LIMIT_BYTES)


def _conv_module(ubuf, gbuf, w_ref, cb_ref, lg_ref, lb_ref, o_ref):
    tt = o_ref.shape[0]
    shifted_rows = tt + CONV_HALO - SUBLANES
    for s in range(1, SUBLANES):
        ubuf[s, 0:shifted_rows, :] = ubuf[0, s:s + shifted_rows, :]
    first = CONV_HALO - (CONV_WIDTH - 1)
    for r0 in range(0, tt, CONV_CHUNK):
        blocks = []
        for c0 in range(0, B_WIDTH, LANES):
            acc = jnp.zeros((CONV_CHUNK, LANES), F32)
            for k in range(CONV_WIDTH):
                shift = (first + k) % SUBLANES
                base = r0 + first + k - shift
                acc = acc + w_ref[k:k + 1, c0:c0 + LANES] * ubuf[shift, base:base + CONV_CHUNK, c0:c0 + LANES]
            blocks.append(acc)
        y = jnp.concatenate(blocks, axis=1) + cb_ref[...]
        mu = jnp.mean(y, axis=-1, keepdims=True)
        yc = y - mu
        var = jnp.mean(yc * yc, axis=-1, keepdims=True)
        yn = yc * lax.rsqrt(var + EPS) * lg_ref[...] + lb_ref[...]
        o_ref[r0:r0 + CONV_CHUNK, :] = (_silu(yn) * gbuf[r0:r0 + CONV_CHUNK, :]).astype(BF16)


def _inproj_kernel(x_ref, g_ref, w_ref, wak2_ref, wavt_ref, wcvt_ref, bd_ref, aqg_ref, akg_ref, cqg_ref, ckg_ref,
                   cw_ref, cb_ref, lg_ref, lb_ref,
                   aq_o, ak2_o, avt_o, ag_o, yb_o, cq_o, ck_o, cvt_o, cg_o, ubuf, gbuf, *, tiles_per_seq):
    tm = x_ref.shape[0]
    x = x_ref[...]
    ms = jnp.mean(x * x, axis=-1, keepdims=True)
    h = (x * lax.rsqrt(ms + EPS) * g_ref[...]).astype(BF16)

    def proj(off, width):
        return jnp.dot(h, w_ref[:, off:off + width], preferred_element_type=F32)

    def head_norm(y, gain, scale):
        w = y.shape[-1]
        sq = (y * y).astype(BF16)
        ssum = jnp.dot(sq, bd_ref[:w, :w], preferred_element_type=F32)
        return y * lax.rsqrt(ssum * (1.0 / HEAD_DIM) + EPS) * (gain * scale)

    qscale_log2 = HEAD_DIM ** -0.5 * math.log2(math.e)
    nt = (((1,), (1,)), ((), ()))
    chunk_cols = lambda c: slice(c * MXU_COLS, (c + 1) * MXU_COLS)
    seq_tile = pl.program_id(0) % tiles_per_seq

    @pl.when(seq_tile == 0)
    def _():
        ubuf[0, 0:CONV_HALO, :] = jnp.zeros((CONV_HALO, B_WIDTH), F32)

    @pl.when(seq_tile > 0)
    def _():
        ubuf[0, 0:CONV_HALO, :] = ubuf[0, tm:tm + CONV_HALO, :]

    for c in range(B_WIDTH // MXU_COLS):
        a = proj(OFF_BGLU + c * MXU_COLS, MXU_COLS)
        b = proj(OFF_BGLU + B_WIDTH + c * MXU_COLS, MXU_COLS)
        ubuf[0, CONV_HALO:, chunk_cols(c)] = a * _sigmoid(b)
        gbuf[:, chunk_cols(c)] = _silu(proj(OFF_BG + c * MXU_COLS, MXU_COLS))
    _conv_module(ubuf, gbuf, cw_ref, cb_ref, lg_ref, lb_ref, yb_o)

    aq = [proj(OFF_AQ + c * MXU_COLS, MXU_COLS) for c in range(A_WIDTH // MXU_COLS)]
    ak2 = jnp.dot(h, wak2_ref[...], preferred_element_type=F32)
    cq = [proj(OFF_CQ + c * MXU_COLS, MXU_COLS) for c in range(C_WIDTH // MXU_COLS)]
    ck = [proj(OFF_CK + c * MXU_COLS, MXU_COLS) for c in range(C_WIDTH // MXU_COLS)]

    for c in range(A_WIDTH // MXU_COLS):
        ag_o[:, chunk_cols(c)] = _silu(proj(OFF_AG + c * MXU_COLS, MXU_COLS)).astype(BF16)
    for c in range(C_WIDTH // MXU_COLS):
        cg_o[:, chunk_cols(c)] = _silu(proj(OFF_CG + c * MXU_COLS, MXU_COLS)).astype(BF16)
    avt_o[...] = lax.dot_general(wavt_ref[...], h, nt, preferred_element_type=F32).astype(BF16)
    cvt_o[...] = lax.dot_general(wcvt_ref[...], h, nt, preferred_element_type=F32).astype(BF16)

    for c in range(A_WIDTH // MXU_COLS):
        aq_o[:, chunk_cols(c)] = head_norm(aq[c], aqg_ref[...], qscale_log2).astype(BF16)
    ak2_o[...] = head_norm(ak2, akg_ref[...], 1.0).astype(BF16)
    for c in range(C_WIDTH // MXU_COLS):
        cq_o[:, chunk_cols(c)] = head_norm(cq[c], cqg_ref[...], qscale_log2).astype(BF16)
        ck_o[:, chunk_cols(c)] = head_norm(ck[c], ckg_ref[...], 1.0).astype(BF16)


def _inproj(x2, batch, layer, norm_g, w_in, w_ak2, w_avt, w_cvt, bd, aqg, akg, cqg, ckg, conv_w, conv_b, ln_g, ln_b):
    n = x2.shape[0]
    t = n // batch
    tm = min(ROW_TILE, t)
    per = t // tm
    const = lambda i: (0, 0)
    row = lambda i: (i, 0)
    rows = lambda w: (pl.BlockSpec((tm, w), row), jax.ShapeDtypeStruct((n, w), BF16))
    cols = lambda w: (pl.BlockSpec((None, w, tm), lambda i: (i // per, 0, i % per)),
                      jax.ShapeDtypeStruct((batch, w, t), BF16))
    outs = (rows(A_WIDTH), rows(2 * A_KV_WIDTH), cols(A_KV_WIDTH), rows(A_WIDTH), rows(B_WIDTH),
            rows(C_WIDTH), rows(C_WIDTH), cols(C_WIDTH), rows(C_WIDTH))
    resident = lambda shape: pl.BlockSpec(shape, const, pipeline_mode=pl.Buffered(1))
    return pl.pallas_call(
        functools.partial(_inproj_kernel, tiles_per_seq=per),
        grid=(n // tm,),
        in_specs=[
            pl.BlockSpec((tm, D_MODEL), row),
            pl.BlockSpec((1, D_MODEL), const),
            pl.BlockSpec((None, D_MODEL, OFF_MG), lambda i: (layer, 0, 0), pipeline_mode=pl.Buffered(1)),
            resident((D_MODEL, 2 * A_KV_WIDTH)),
            resident((A_KV_WIDTH, D_MODEL)),
            resident((C_WIDTH, D_MODEL)),
            pl.BlockSpec((MXU_COLS, MXU_COLS), const),
            pl.BlockSpec((1, MXU_COLS), const),
            pl.BlockSpec((1, MXU_COLS), const),
            pl.BlockSpec((1, MXU_COLS), const),
            pl.BlockSpec((1, MXU_COLS), const),
            pl.BlockSpec((CONV_WIDTH, B_WIDTH), const),
            pl.BlockSpec((1, B_WIDTH), const),
            pl.BlockSpec((1, B_WIDTH), const),
            pl.BlockSpec((1, B_WIDTH), const),
        ],
        out_specs=[o[0] for o in outs],
        out_shape=[o[1] for o in outs],
        scratch_shapes=[pltpu.VMEM((SUBLANES, tm + CONV_HALO, B_WIDTH), F32), pltpu.VMEM((tm, B_WIDTH), F32)],
        compiler_params=_params(("arbitrary",)),
        name="inproj",
    )(x2, norm_g, w_in, w_ak2, w_avt, w_cvt, bd, aqg, akg, cqg, ckg, conv_w, conv_b, ln_g, ln_b)


def _swa_kernel(sink_ref, q_ref, kc_ref, kp_ref, vtc_ref, vtp_ref, ag_ref, o_ref):
    i = pl.program_id(1)
    nblk = q_ref.shape[0] // WINDOW
    kfull = jnp.concatenate([kp_ref[...], kc_ref[...]], axis=0)
    vtfull = jnp.concatenate([vtp_ref[...], vtc_ref[...]], axis=1)
    pair = 2 * WINDOW
    c = lax.broadcasted_iota(jnp.int32, (pair, pair), 0)
    t = lax.broadcasted_iota(jnp.int32, (pair, pair), 1) & (WINDOW - 1)
    local = (c > t) & (c <= t + WINDOW)
    first_col = lax.broadcasted_iota(jnp.int32, (1, pair), 1) < WINDOW
    half = lax.broadcasted_iota(jnp.int32, (WINDOW, LANES), 1) < HEAD_DIM
    ones_rows = jnp.ones((DIFF_SUM_ROWS, pair), BF16)
    log2e = math.log2(math.e)
    chains = [(j, kv, par) for j in range(nblk) for kv in range(A_KV_HEADS) for par in range(2)]

    def scores(j, kv, par):
        q = q_ref[j * WINDOW:(j + 1) * WINDOW, :]
        kwin = kfull[j * WINDOW:(j + 2) * WINDOW, :]
        qz = []
        for h in (kv * A_GROUP + par, kv * A_GROUP + par + 2):
            blk = q[:, (h // 2) * LANES:(h // 2 + 1) * LANES]
            keep = half if par == 0 else jnp.logical_not(half)
            qz.append(jnp.where(keep, blk, jnp.zeros_like(blk)))
        qz = jnp.concatenate(qz, axis=0)
        ksel = kwin[:, :LANES] if par == kv else kwin[:, LANES:]
        return lax.dot_general(ksel, qz, (((1,), (1,)), ((), ())), preferred_element_type=F32)

    s_all = [scores(*ch) for ch in chains]
    pieces = {}
    for (j, kv, par), s in zip(chains, s_all):
        heads = (kv * A_GROUP + par, kv * A_GROUP + par + 2)
        mask = local & ((c >= WINDOW) | (i > 0)) if j == 0 else local
        s = jnp.where(mask, s, -jnp.inf)
        sink = jnp.where(first_col, sink_ref[heads[0]] * log2e, sink_ref[heads[1]] * log2e)
        m = jnp.maximum(jnp.max(s, axis=0, keepdims=True), sink)
        p = jnp.exp2(s - m).astype(BF16)
        vtwin = vtfull[kv * HEAD_DIM:(kv + 1) * HEAD_DIM, j * WINDOW:(j + 2) * WINDOW]
        acc = jnp.dot(jnp.concatenate([vtwin, ones_rows], axis=0), p, preferred_element_type=F32)
        denom = acc[HEAD_DIM:HEAD_DIM + 1, :] + jnp.exp2(sink - m)
        o = acc[:HEAD_DIM, :] * (1.0 / denom)
        pieces[j, heads[0]] = o[:, :WINDOW]
        pieces[j, heads[1]] = o[:, WINDOW:]
    for j in range(nblk):
        cols = [jnp.concatenate([pieces[j, 2 * cb], pieces[j, 2 * cb + 1]], axis=0).T for cb in range(A_Q_HEADS // 2)]
        o_all = jnp.concatenate(cols, axis=1)
        gate = ag_ref[j * WINDOW:(j + 1) * WINDOW, :].astype(F32)
        o_ref[j * WINDOW:(j + 1) * WINDOW, :] = (o_all * gate).astype(BF16)


def _swa(sinks, aq, ak2, avt, ag):
    b, t, _ = aq.shape
    tq = min(SWA_TILE, t)
    per = tq // WINDOW
    cur = lambda bi, i: (bi, i, 0)
    prev = lambda bi, i: (bi, jnp.maximum(i * per - 1, 0), 0)
    cur_t = lambda bi, i: (bi, 0, i)
    prev_t = lambda bi, i: (bi, 0, jnp.maximum(i * per - 1, 0))
    return pl.pallas_call(
        _swa_kernel,
        grid=(b, t // tq),
        in_specs=[
            pl.BlockSpec(memory_space=pltpu.SMEM),
            pl.BlockSpec((None, tq, A_WIDTH), cur),
            pl.BlockSpec((None, tq, 2 * A_KV_WIDTH), cur),
            pl.BlockSpec((None, WINDOW, 2 * A_KV_WIDTH), prev),
            pl.BlockSpec((None, A_KV_WIDTH, tq), cur_t),
            pl.BlockSpec((None, A_KV_WIDTH, WINDOW), prev_t),
            pl.BlockSpec((None, tq, A_WIDTH), cur),
        ],
        out_specs=pl.BlockSpec((None, tq, A_WIDTH), cur),
        out_shape=jax.ShapeDtypeStruct((b, t, A_WIDTH), BF16),
        compiler_params=_params(("parallel", "parallel")),
        name="swa",
    )(sinks, aq, ak2, ak2, avt, avt, ag)


def _diff_kernel(lq_ref, lk_ref, q_ref, k_ref, vt_ref, cg_ref, sg_ref, o_ref, *, lam_init):
    i = pl.program_id(2)
    tq = q_ref.shape[0]
    tk = DIFF_TK
    heads = q_ref.shape[1] // C_HEAD_WIDTH
    e = jnp.exp(jnp.sum(lq_ref[...] * lk_ref[...], axis=-1, keepdims=True))
    lam = e[0:1, :] - e[1:2, :] + lam_init

    lane = lax.broadcasted_iota(jnp.int32, (tq, C_HEAD_WIDTH), 1)
    qs = []
    for h in range(heads):
        q = q_ref[:, h * C_HEAD_WIDTH:(h + 1) * C_HEAD_WIDTH]
        zero = jnp.zeros_like(q)
        qs.append(jnp.concatenate([jnp.where(lane < HEAD_DIM, q, zero), jnp.where(lane >= HEAD_DIM, q, zero)],
                                  axis=0))

    def scores(h, start, size):
        kt = k_ref[pl.ds(start, size), h * C_HEAD_WIDTH:(h + 1) * C_HEAD_WIDTH]
        return lax.dot_general(kt, qs[h], (((1,), (1,)), ((), ())), preferred_element_type=F32)

    def head_step(h, start, size, s, carry, masked):
        m, acc = carry
        vt = vt_ref[h * C_HEAD_WIDTH:(h + 1) * C_HEAD_WIDTH, pl.ds(start, size)]
        vt = jnp.concatenate([vt, jnp.ones((DIFF_SUM_ROWS, size), BF16)], axis=0)
        if masked:
            kpos = lax.broadcasted_iota(jnp.int32, s.shape, 0)
            qpos = lax.broadcasted_iota(jnp.int32, s.shape, 1) & (tq - 1)
            s = jnp.where(kpos <= qpos, s, -jnp.inf)
        m_new = jnp.maximum(m, jnp.max(s, axis=0, keepdims=True))
        alpha = jnp.exp2(m - m_new)
        p = jnp.exp2(s - m_new).astype(BF16)
        acc = alpha * acc + jnp.dot(vt, p, preferred_element_type=F32)
        return m_new, acc

    def step(start, size, carries, masked):
        s_all = [scores(h, start, size) for h in range(heads)]
        return tuple(head_step(h, start, size, s_all[h], carries[h], masked) for h in range(heads))

    init = (jnp.full((1, 2 * tq), -jnp.inf, F32), jnp.zeros((C_HEAD_WIDTH + DIFF_SUM_ROWS, 2 * tq), F32))
    carries = lax.fori_loop(
        0, i // 2, lambda jj, c: step(pl.multiple_of(jj * (2 * tk), 2 * tk), 2 * tk, c, False), (init,) * heads)
    carries = lax.cond(i % 2 == 1, lambda c: step(pl.multiple_of((i - 1) * tk, tk), tk, c, False), lambda c: c,
                       carries)
    carries = step(pl.multiple_of(i * tk, tk), tk, carries, True)
    for h in range(heads):
        _, acc = carries[h]
        o = acc[:C_HEAD_WIDTH, :] * (1.0 / acc[C_HEAD_WIDTH:C_HEAD_WIDTH + 1, :])
        d = o[:, :tq] - lam * o[:, tq:]
        ms = jnp.mean(d * d, axis=0, keepdims=True)
        y = (d * lax.rsqrt(ms + EPS)).T
        y = y * (sg_ref[...] * (1.0 - lam_init))
        sl = slice(h * C_HEAD_WIDTH, (h + 1) * C_HEAD_WIDTH)
        o_ref[:, sl] = (y * cg_ref[:, sl].astype(F32)).astype(BF16)


def _diff(lq, lk, cq, ck, cvt, cg, subln_g, lam_init):
    b, t, _ = cq.shape
    tq = min(DIFF_TQ, t)
    assert tq == DIFF_TK
    width = DIFF_HEADS_PER_STEP * C_HEAD_WIDTH
    qmap = lambda bi, h, i: (bi, i, h)
    kmap = lambda bi, h, i: (bi, 0, h)
    vtmap = lambda bi, h, i: (bi, h, 0)
    const = lambda bi, h, i: (0, 0)
    return pl.pallas_call(
        functools.partial(_diff_kernel, lam_init=lam_init),
        grid=(b, C_HEADS // DIFF_HEADS_PER_STEP, t // tq),
        in_specs=[
            pl.BlockSpec((2, HEAD_DIM), const),
            pl.BlockSpec((2, HEAD_DIM), const),
            pl.BlockSpec((None, tq, width), qmap),
            pl.BlockSpec((None, t, width), kmap),
            pl.BlockSpec((None, width, t), vtmap),
            pl.BlockSpec((None, tq, width), qmap),
            pl.BlockSpec((1, C_HEAD_WIDTH), const),
        ],
        out_specs=pl.BlockSpec((None, tq, width), qmap),
        out_shape=jax.ShapeDtypeStruct((b, t, C_WIDTH), BF16),
        compiler_params=_params(("parallel", "parallel", "arbitrary")),
        name="diff",
    )(lq, lk, cq, ck, cvt, cg, subln_g)


def _merge_kernel(x_ref, g_ref, ya_ref, yb_ref, yc_ref, wg_ref, wa_ref, wb_ref, wc_ref, wo_ref, o_ref, merged):
    x = x_ref[...]
    ms = jnp.mean(x * x, axis=-1, keepdims=True)
    h = (x * lax.rsqrt(ms + EPS) * g_ref[...]).astype(BF16)
    branches = ((ya_ref, wa_ref), (yb_ref, wb_ref), (yc_ref, wc_ref))
    for c in range(D_MODEL // MXU_COLS):
        sl = slice(c * MXU_COLS, (c + 1) * MXU_COLS)
        total = None
        for br, (y_ref, w_ref) in enumerate(branches):
            gate_cols = slice(br * D_MODEL + c * MXU_COLS, br * D_MODEL + (c + 1) * MXU_COLS)
            gate = _sigmoid(jnp.dot(h, wg_ref[:, gate_cols], preferred_element_type=F32))
            term = gate * jnp.dot(y_ref[...], w_ref[:, sl], preferred_element_type=F32)
            total = term if total is None else total + term
        merged[:, sl] = total.astype(BF16)
    o_ref[...] = x + jnp.dot(merged[...], wo_ref[...], preferred_element_type=F32)


def _merge(x2, norm_g, ya, yb, yc, wg, wa, wb, wc, wo):
    n = x2.shape[0]
    tm = min(MERGE_TILE, n)
    row = lambda i: (i, 0)
    const = lambda i: (0, 0)
    resident = lambda shape: pl.BlockSpec(shape, const, pipeline_mode=pl.Buffered(1))
    return pl.pallas_call(
        _merge_kernel,
        grid=(n // tm,),
        in_specs=[
            pl.BlockSpec((tm, D_MODEL), row),
            pl.BlockSpec((1, D_MODEL), const),
            pl.BlockSpec((tm, A_WIDTH), row),
            pl.BlockSpec((tm, B_WIDTH), row),
            pl.BlockSpec((tm, C_WIDTH), row),
            resident((D_MODEL, MERGE_COLS)),
            resident((A_WIDTH, D_MODEL)),
            resident((B_WIDTH, D_MODEL)),
            resident((C_WIDTH, D_MODEL)),
            resident((D_MODEL, D_MODEL)),
        ],
        out_specs=pl.BlockSpec((tm, D_MODEL), row),
        out_shape=jax.ShapeDtypeStruct((n, D_MODEL), F32),
        scratch_shapes=[pltpu.VMEM((tm, D_MODEL), BF16)],
        compiler_params=_params(("parallel",)),
        name="merge",
    )(x2, norm_g, ya, yb, yc, wg, wa, wb, wc, wo)


def _cast_kernel(w_ref, o_ref):
    o_ref[...] = w_ref[...].astype(BF16)


def _cast_bf16(w):
    depth, rows, cols = w.shape
    tr = min(CAST_ROWS, rows)
    spec = pl.BlockSpec((None, tr, cols), lambda l, i: (l, i, 0))
    return pl.pallas_call(
        _cast_kernel,
        grid=(depth, rows // tr),
        in_specs=[spec],
        out_specs=spec,
        out_shape=jax.ShapeDtypeStruct(w.shape, BF16),
        compiler_params=_params(("parallel", "parallel")),
        name="cast",
    )(w)


def kernel(x, norm_g, w_in, attn_q_norm_g, attn_k_norm_g, attn_sinks, w_o_attn, conv_w, conv_b, conv_norm_g, conv_norm_b, w_o_conv, diff_q_norm_g, diff_k_norm_g, lambda_q, lambda_k, diff_subln_g, w_o_diff, w_out):
    b, t, d = x.shape
    n = b * t
    depth = norm_g.shape[0]
    group = lax.broadcasted_iota(jnp.int32, (MXU_COLS, MXU_COLS), 0) // HEAD_DIM
    bd = (group == group.T).astype(BF16)
    tile_gain = lambda g: jnp.tile(g.astype(F32), MXU_COLS // HEAD_DIM).reshape(1, MXU_COLS)
    row = lambda v: v.astype(F32).reshape(1, -1)

    x2 = x.reshape(n, d)
    w_bf = _cast_bf16(w_in)
    for l in range(depth):
        w_k0 = w_bf[l, :, OFF_AK:OFF_AK + HEAD_DIM]
        w_k1 = w_bf[l, :, OFF_AK + HEAD_DIM:OFF_AK + A_KV_WIDTH]
        aq, ak2, avt, ag, yb, cq, ck, cvt, cg = _inproj(
            x2, b, l, row(norm_g[l]), w_bf, jnp.concatenate([w_k0, w_k1, w_k1, w_k0], axis=1),
            w_bf[l, :, OFF_AV:OFF_AV + A_KV_WIDTH].T, w_bf[l, :, OFF_CV:OFF_CV + C_WIDTH].T, bd,
            tile_gain(attn_q_norm_g[l]), tile_gain(attn_k_norm_g[l]),
            tile_gain(diff_q_norm_g[l]), tile_gain(diff_k_norm_g[l]),
            conv_w[l].astype(F32), row(conv_b[l]), row(conv_norm_g[l]), row(conv_norm_b[l]))
        r3 = lambda a: a.reshape(b, t, a.shape[-1])
        ya = _swa(attn_sinks[l].astype(F32), r3(aq), r3(ak2), avt, r3(ag))
        lam_init = 0.8 - 0.6 * math.exp(-0.3 * l)
        yc = _diff(lambda_q[l].astype(F32), lambda_k[l].astype(F32), r3(cq), r3(ck), cvt, r3(cg),
                   row(diff_subln_g[l]), lam_init)
        x2 = _merge(x2, row(norm_g[l]), ya.reshape(n, -1), yb, yc.reshape(n, -1), w_bf[l, :, OFF_MG:],
                    w_o_attn[l].astype(BF16), w_o_conv[l].astype(BF16), w_o_diff[l].astype(BF16),
                    w_out[l].astype(BF16))
    return x2.reshape(b, t, d)
```

```python
import functools
import math

import jax
import jax.numpy as jnp
from jax import lax
from jax.experimental import pallas as pl
from jax.experimental.pallas import tpu as pltpu

F32 = jnp.float32
BF16 = jnp.bfloat16

D_MODEL = 1024
HEAD_DIM = 64
EPS = 1e-6
A_Q_HEADS = 8
A_KV_HEADS = 2
A_GROUP = A_Q_HEADS // A_KV_HEADS
A_WIDTH = A_Q_HEADS * HEAD_DIM
A_KV_WIDTH = A_KV_HEADS * HEAD_DIM
WINDOW = 128
B_WIDTH = D_MODEL // 2
CONV_WIDTH = 31
C_HEADS = 4
C_HEAD_WIDTH = 2 * HEAD_DIM
C_WIDTH = C_HEADS * C_HEAD_WIDTH
N_BRANCH = 3
MERGE_COLS = N_BRANCH * D_MODEL

OFF_AQ = 0
OFF_AK = OFF_AQ + A_WIDTH
OFF_AV = OFF_AK + A_KV_WIDTH
OFF_AG = OFF_AV + A_KV_WIDTH
OFF_BGLU = OFF_AG + A_WIDTH
OFF_BG = OFF_BGLU + 2 * B_WIDTH
OFF_CQ = OFF_BG + B_WIDTH
OFF_CK = OFF_CQ + C_WIDTH
OFF_CV = OFF_CK + C_WIDTH
OFF_CG = OFF_CV + C_WIDTH
OFF_MG = OFF_CG + C_WIDTH
IN_WIDTH = OFF_MG + MERGE_COLS

MXU_COLS = 256
LANES = 128
SUBLANES = 8
VMEM_LIMIT_BYTES = 56 * 1024 * 1024

ROW_TILE = 512
MERGE_TILE = 1024
CAST_ROWS = 256
SWA_TILE = 2048
CONV_HALO = 32
CONV_CHUNK = 64
DIFF_TQ = 512
DIFF_TK = 512
DIFF_HEADS_PER_STEP = 4
DIFF_SCORE_LEAD = 2
DIFF_SUM_ROWS = 16


def _sigmoid(y):
    return 1.0 / (1.0 + jnp.exp(-y))


def _silu(y):
    return y * _sigmoid(y)


def _params(semantics):
    return pltpu.CompilerParams(dimension_semantics=semantics, vmem_limit_bytes=VMEM_LIMIT_BYTES)


def _conv_module(ubuf, gbuf, w_ref, cb_ref, lg_ref, lb_ref, o_ref):
    tt = o_ref.shape[0]
    shifted_rows = tt + CONV_HALO - SUBLANES
    for s in range(1, SUBLANES):
        ubuf[s, 0:shifted_rows, :] = ubuf[0, s:s + shifted_rows, :]
    first = CONV_HALO - (CONV_WIDTH - 1)
    for r0 in range(0, tt, CONV_CHUNK):
        blocks = []
        for c0 in range(0, B_WIDTH, LANES):
            acc = jnp.zeros((CONV_CHUNK, LANES), F32)
            for k in range(CONV_WIDTH):
                shift = (first + k) % SUBLANES
                base = r0 + first + k - shift
                acc = acc + w_ref[k:k + 1, c0:c0 + LANES] * ubuf[shift, base:base + CONV_CHUNK, c0:c0 + LANES]
            blocks.append(acc)
        y = jnp.concatenate(blocks, axis=1) + cb_ref[...]
        mu = jnp.mean(y, axis=-1, keepdims=True)
        yc = y - mu
        var = jnp.mean(yc * yc, axis=-1, keepdims=True)
        yn = yc * lax.rsqrt(var + EPS) * lg_ref[...] + lb_ref[...]
        o_ref[r0:r0 + CONV_CHUNK, :] = (_silu(yn) * gbuf[r0:r0 + CONV_CHUNK, :]).astype(BF16)


def _inproj_kernel(x_ref, g_ref, w_ref, wak2_ref, wavt_ref, wcvt_ref, bd_ref, aqg_ref, akg_ref, cqg_ref, ckg_ref,
                   cw_ref, cb_ref, lg_ref, lb_ref,
                   aq_o, ak2_o, avt_o, ag_o, yb_o, cq_o, ck_o, cvt_o, cg_o, ubuf, gbuf, *, tiles_per_seq):
    tm = x_ref.shape[0]
    x = x_ref[...]
    ms = jnp.mean(x * x, axis=-1, keepdims=True)
    h = (x * lax.rsqrt(ms + EPS) * g_ref[...]).astype(BF16)

    def proj(off, width):
        return jnp.dot(h, w_ref[:, off:off + width], preferred_element_type=F32)

    def head_norm(y, gain, scale):
        w = y.shape[-1]
        sq = (y * y).astype(BF16)
        ssum = jnp.dot(sq, bd_ref[:w, :w], preferred_element_type=F32)
        return y * lax.rsqrt(ssum * (1.0 / HEAD_DIM) + EPS) * (gain * scale)

    qscale_log2 = HEAD_DIM ** -0.5 * math.log2(math.e)
    nt = (((1,), (1,)), ((), ()))
    chunk_cols = lambda c: slice(c * MXU_COLS, (c + 1) * MXU_COLS)
    seq_tile = pl.program_id(0) % tiles_per_seq

    @pl.when(seq_tile == 0)
    def _():
        ubuf[0, 0:CONV_HALO, :] = jnp.zeros((CONV_HALO, B_WIDTH), F32)

    @pl.when(seq_tile > 0)
    def _():
        ubuf[0, 0:CONV_HALO, :] = ubuf[0, tm:tm + CONV_HALO, :]

    for c in range(B_WIDTH // MXU_COLS):
        a = proj(OFF_BGLU + c * MXU_COLS, MXU_COLS)
        b = proj(OFF_BGLU + B_WIDTH + c * MXU_COLS, MXU_COLS)
        ubuf[0, CONV_HALO:, chunk_cols(c)] = a * _sigmoid(b)
        gbuf[:, chunk_cols(c)] = _silu(proj(OFF_BG + c * MXU_COLS, MXU_COLS))
    _conv_module(ubuf, gbuf, cw_ref, cb_ref, lg_ref, lb_ref, yb_o)

    aq = [proj(OFF_AQ + c * MXU_COLS, MXU_COLS) for c in range(A_WIDTH // MXU_COLS)]
    ak2 = jnp.dot(h, wak2_ref[...], preferred_element_type=F32)
    cq = [proj(OFF_CQ + c * MXU_COLS, MXU_COLS) for c in range(C_WIDTH // MXU_COLS)]
    ck = [proj(OFF_CK + c * MXU_COLS, MXU_COLS) for c in range(C_WIDTH // MXU_COLS)]

    for c in range(A_WIDTH // MXU_COLS):
        ag_o[:, chunk_cols(c)] = _silu(proj(OFF_AG + c * MXU_COLS, MXU_COLS)).astype(BF16)
    for c in range(C_WIDTH // MXU_COLS):
        cg_o[:, chunk_cols(c)] = _silu(proj(OFF_CG + c * MXU_COLS, MXU_COLS)).astype(BF16)
    avt_o[...] = lax.dot_general(wavt_ref[...], h, nt, preferred_element_type=F32).astype(BF16)
    cvt_o[...] = lax.dot_general(wcvt_ref[...], h, nt, preferred_element_type=F32).astype(BF16)

    for c in range(A_WIDTH // MXU_COLS):
        aq_o[:, chunk_cols(c)] = head_norm(aq[c], aqg_ref[...], qscale_log2).astype(BF16)
    ak2_o[...] = head_norm(ak2, akg_ref[...], 1.0).astype(BF16)
    for c in range(C_WIDTH // MXU_COLS):
        cq_o[:, chunk_cols(c)] = head_norm(cq[c], cqg_ref[...], qscale_log2).astype(BF16)
        ck_o[:, chunk_cols(c)] = head_norm(ck[c], ckg_ref[...], 1.0).astype(BF16)


def _inproj(x2, batch, layer, norm_g, w_in, w_ak2, w_avt, w_cvt, bd, aqg, akg, cqg, ckg, conv_w, conv_b, ln_g, ln_b):
    n = x2.shape[0]
    t = n // batch
    tm = min(ROW_TILE, t)
    per = t // tm
    const = lambda i: (0, 0)
    row = lambda i: (i, 0)
    rows = lambda w: (pl.BlockSpec((tm, w), row), jax.ShapeDtypeStruct((n, w), BF16))
    cols = lambda w: (pl.BlockSpec((None, w, tm), lambda i: (i // per, 0, i % per)),
                      jax.ShapeDtypeStruct((batch, w, t), BF16))
    outs = (rows(A_WIDTH), rows(2 * A_KV_WIDTH), cols(A_KV_WIDTH), rows(A_WIDTH), rows(B_WIDTH),
            rows(C_WIDTH), rows(C_WIDTH), cols(C_WIDTH), rows(C_WIDTH))
    resident = lambda shape: pl.BlockSpec(shape, const, pipeline_mode=pl.Buffered(1))
    return pl.pallas_call(
        functools.partial(_inproj_kernel, tiles_per_seq=per),
        grid=(n // tm,),
        in_specs=[
            pl.BlockSpec((tm, D_MODEL), row),
            pl.BlockSpec((1, D_MODEL), const),
            pl.BlockSpec((None, D_MODEL, OFF_MG), lambda i: (layer, 0, 0), pipeline_mode=pl.Buffered(1)),
            resident((D_MODEL, 2 * A_KV_WIDTH)),
            resident((A_KV_WIDTH, D_MODEL)),
            resident((C_WIDTH, D_MODEL)),
            pl.BlockSpec((MXU_COLS, MXU_COLS), const),
            pl.BlockSpec((1, MXU_COLS), const),
            pl.BlockSpec((1, MXU_COLS), const),
            pl.BlockSpec((1, MXU_COLS), const),
            pl.BlockSpec((1, MXU_COLS), const),
            pl.BlockSpec((CONV_WIDTH, B_WIDTH), const),
            pl.BlockSpec((1, B_WIDTH), const),
            pl.BlockSpec((1, B_WIDTH), const),
            pl.BlockSpec((1, B_WIDTH), const),
        ],
        out_specs=[o[0] for o in outs],
        out_shape=[o[1] for o in outs],
        scratch_shapes=[pltpu.VMEM((SUBLANES, tm + CONV_HALO, B_WIDTH), F32), pltpu.VMEM((tm, B_WIDTH), F32)],
        compiler_params=_params(("arbitrary",)),
        name="inproj",
    )(x2, norm_g, w_in, w_ak2, w_avt, w_cvt, bd, aqg, akg, cqg, ckg, conv_w, conv_b, ln_g, ln_b)


def _swa_kernel(sink_ref, q_ref, kc_ref, kp_ref, vtc_ref, vtp_ref, ag_ref, o_ref):
    i = pl.program_id(1)
    nblk = q_ref.shape[0] // WINDOW
    kfull = jnp.concatenate([kp_ref[...], kc_ref[...]], axis=0)
    vtfull = jnp.concatenate([vtp_ref[...], vtc_ref[...]], axis=1)
    pair = 2 * WINDOW
    c = lax.broadcasted_iota(jnp.int32, (pair, pair), 0)
    t = lax.broadcasted_iota(jnp.int32, (pair, pair), 1) & (WINDOW - 1)
    local = (c > t) & (c <= t + WINDOW)
    first_col = lax.broadcasted_iota(jnp.int32, (1, pair), 1) < WINDOW
    half = lax.broadcasted_iota(jnp.int32, (WINDOW, LANES), 1) < HEAD_DIM
    ones_rows = jnp.ones((DIFF_SUM_ROWS, pair), BF16)
    log2e = math.log2(math.e)
    chains = [(j, kv, par) for j in range(nblk) for kv in range(A_KV_HEADS) for par in range(2)]

    def scores(j, kv, par):
        q = q_ref[j * WINDOW:(j + 1) * WINDOW, :]
        kwin = kfull[j * WINDOW:(j + 2) * WINDOW, :]
        qz = []
        for h in (kv * A_GROUP + par, kv * A_GROUP + par + 2):
            blk = q[:, (h // 2) * LANES:(h // 2 + 1) * LANES]
            keep = half if par == 0 else jnp.logical_not(half)
            qz.append(jnp.where(keep, blk, jnp.zeros_like(blk)))
        qz = jnp.concatenate(qz, axis=0)
        ksel = kwin[:, :LANES] if par == kv else kwin[:, LANES:]
        return lax.dot_general(ksel, qz, (((1,), (1,)), ((), ())), preferred_element_type=F32)

    s_all = [scores(*ch) for ch in chains]
    pieces = {}
    for (j, kv, par), s in zip(chains, s_all):
        heads = (kv * A_GROUP + par, kv * A_GROUP + par + 2)
        mask = local & ((c >= WINDOW) | (i > 0)) if j == 0 else local
        s = jnp.where(mask, s, -jnp.inf)
        sink = jnp.where(first_col, sink_ref[heads[0]] * log2e, sink_ref[heads[1]] * log2e)
        m = jnp.maximum(jnp.max(s, axis=0, keepdims=True), sink)
        p = jnp.exp2(s - m).astype(BF16)
        vtwin = vtfull[kv * HEAD_DIM:(kv + 1) * HEAD_DIM, j * WINDOW:(j + 2) * WINDOW]
        acc = jnp.dot(jnp.concatenate([vtwin, ones_rows], axis=0), p, preferred_element_type=F32)
        denom = acc[HEAD_DIM:HEAD_DIM + 1, :] + jnp.exp2(sink - m)
        o = acc[:HEAD_DIM, :] * (1.0 / denom)
        pieces[j, heads[0]] = o[:, :WINDOW]
        pieces[j, heads[1]] = o[:, WINDOW:]
    for j in range(nblk):
        cols = [jnp.concatenate([pieces[j, 2 * cb], pieces[j, 2 * cb + 1]], axis=0).T for cb in range(A_Q_HEADS // 2)]
        o_all = jnp.concatenate(cols, axis=1)
        gate = ag_ref[j * WINDOW:(j + 1) * WINDOW, :].astype(F32)
        o_ref[j * WINDOW:(j + 1) * WINDOW, :] = (o_all * gate).astype(BF16)


def _swa(sinks, aq, ak2, avt, ag):
    b, t, _ = aq.shape
    tq = min(SWA_TILE, t)
    per = tq // WINDOW
    cur = lambda bi, i: (bi, i, 0)
    prev = lambda bi, i: (bi, jnp.maximum(i * per - 1, 0), 0)
    cur_t = lambda bi, i: (bi, 0, i)
    prev_t = lambda bi, i: (bi, 0, jnp.maximum(i * per - 1, 0))
    return pl.pallas_call(
        _swa_kernel,
        grid=(b, t // tq),
        in_specs=[
            pl.BlockSpec(memory_space=pltpu.SMEM),
            pl.BlockSpec((None, tq, A_WIDTH), cur),
            pl.BlockSpec((None, tq, 2 * A_KV_WIDTH), cur),
            pl.BlockSpec((None, WINDOW, 2 * A_KV_WIDTH), prev),
            pl.BlockSpec((None, A_KV_WIDTH, tq), cur_t),
            pl.BlockSpec((None, A_KV_WIDTH, WINDOW), prev_t),
            pl.BlockSpec((None, tq, A_WIDTH), cur),
        ],
        out_specs=pl.BlockSpec((None, tq, A_WIDTH), cur),
        out_shape=jax.ShapeDtypeStruct((b, t, A_WIDTH), BF16),
        compiler_params=_params(("parallel", "parallel")),
        name="swa",
    )(sinks, aq, ak2, ak2, avt, avt, ag)


def _diff_kernel(lq_ref, lk_ref, q_ref, k_ref, vt_ref, cg_ref, sg_ref, o_ref, *, lam_init):
    i = pl.program_id(2)
    tq = q_ref.shape[0]
    tk = DIFF_TK
    heads = q_ref.shape[1] // C_HEAD_WIDTH
    e = jnp.exp(jnp.sum(lq_ref[...] * lk_ref[...], axis=-1, keepdims=True))
    lam = e[0:1, :] - e[1:2, :] + lam_init

    lane = lax.broadcasted_iota(jnp.int32, (tq, C_HEAD_WIDTH), 1)
    qs = []
    for h in range(heads):
        q = q_ref[:, h * C_HEAD_WIDTH:(h + 1) * C_HEAD_WIDTH]
        zero = jnp.zeros_like(q)
        qs.append(jnp.concatenate([jnp.where(lane < HEAD_DIM, q, zero), jnp.where(lane >= HEAD_DIM, q, zero)],
                                  axis=0))

    def scores(h, start, size):
        kt = k_ref[pl.ds(start, size), h * C_HEAD_WIDTH:(h + 1) * C_HEAD_WIDTH]
        return lax.dot_general(kt, qs[h], (((1,), (1,)), ((), ())), preferred_element_type=F32)

    def head_step(h, start, size, s, carry, masked):
        m, acc = carry
        vt = vt_ref[h * C_HEAD_WIDTH:(h + 1) * C_HEAD_WIDTH, pl.ds(start, size)]
        vt = jnp.concatenate([vt, jnp.ones((DIFF_SUM_ROWS, size), BF16)], axis=0)
        if masked:
            kpos = lax.broadcasted_iota(jnp.int32, s.shape, 0)
            qpos = lax.broadcasted_iota(jnp.int32, s.shape, 1) & (tq - 1)
            s = jnp.where(kpos <= qpos, s, -jnp.inf)
        m_new = jnp.maximum(m, jnp.max(s, axis=0, keepdims=True))
        alpha = jnp.exp2(m - m_new)
        p = jnp.exp2(s - m_new).astype(BF16)
        acc = alpha * acc + jnp.dot(vt, p, preferred_element_type=F32)
        return m_new, acc

    def step(start, size, carries, masked):
        lead = min(DIFF_SCORE_LEAD, heads)
        s_all = [scores(h, start, size) for h in range(lead)]
        out = []
        for h in range(heads):
            if h + lead < heads:
                s_all.append(scores(h + lead, start, size))
            out.append(head_step(h, start, size, s_all[h], carries[h], masked))
        return tuple(out)

    init = (jnp.full((1, 2 * tq), -jnp.inf, F32), jnp.zeros((C_HEAD_WIDTH + DIFF_SUM_ROWS, 2 * tq), F32))
    carries = lax.fori_loop(
        0, i // 2, lambda jj, c: step(pl.multiple_of(jj * (2 * tk), 2 * tk), 2 * tk, c, False), (init,) * heads)
    carries = lax.cond(i % 2 == 1, lambda c: step(pl.multiple_of((i - 1) * tk, tk), tk, c, False), lambda c: c,
                       carries)
    carries = step(pl.multiple_of(i * tk, tk), tk, carries, True)
    for h in range(heads):
        _, acc = carries[h]
        o = acc[:C_HEAD_WIDTH, :] * (1.0 / acc[C_HEAD_WIDTH:C_HEAD_WIDTH + 1, :])
        d = o[:, :tq] - lam * o[:, tq:]
        ms = jnp.mean(d * d, axis=0, keepdims=True)
        y = (d * lax.rsqrt(ms + EPS)).T
        y = y * (sg_ref[...] * (1.0 - lam_init))
        sl = slice(h * C_HEAD_WIDTH, (h + 1) * C_HEAD_WIDTH)
        o_ref[:, sl] = (y * cg_ref[:, sl].astype(F32)).astype(BF16)


def _diff(lq, lk, cq, ck, cvt, cg, subln_g, lam_init):
    b, t, _ = cq.shape
    tq = min(DIFF_TQ, t)
    assert tq == DIFF_TK
    width = DIFF_HEADS_PER_STEP * C_HEAD_WIDTH
    qmap = lambda bi, h, i: (bi, i, h)
    kmap = lambda bi, h, i: (bi, 0, h)
    vtmap = lambda bi, h, i: (bi, h, 0)
    const = lambda bi, h, i: (0, 0)
    return pl.pallas_call(
        functools.partial(_diff_kernel, lam_init=lam_init),
        grid=(b, C_HEADS // DIFF_HEADS_PER_STEP, t // tq),
        in_specs=[
            pl.BlockSpec((2, HEAD_DIM), const),
            pl.BlockSpec((2, HEAD_DIM), const),
            pl.BlockSpec((None, tq, width), qmap),
            pl.BlockSpec((None, t, width), kmap),
            pl.BlockSpec((None, width, t), vtmap),
            pl.BlockSpec((None, tq, width), qmap),
            pl.BlockSpec((1, C_HEAD_WIDTH), const),
        ],
        out_specs=pl.BlockSpec((None, tq, width), qmap),
        out_shape=jax.ShapeDtypeStruct((b, t, C_WIDTH), BF16),
        compiler_params=_params(("parallel", "parallel", "arbitrary")),
        name="diff",
    )(lq, lk, cq, ck, cvt, cg, subln_g)


def _merge_kernel(x_ref, g_ref, ya_ref, yb_ref, yc_ref, wg_ref, wa_ref, wb_ref, wc_ref, wo_ref, o_ref, merged):
    x = x_ref[...]
    ms = jnp.mean(x * x, axis=-1, keepdims=True)
    h = (x * lax.rsqrt(ms + EPS) * g_ref[...]).astype(BF16)
    branches = ((ya_ref, wa_ref), (yb_ref, wb_ref), (yc_ref, wc_ref))
    for c in range(D_MODEL // MXU_COLS):
        sl = slice(c * MXU_COLS, (c + 1) * MXU_COLS)
        total = None
        for br, (y_ref, w_ref) in enumerate(branches):
            gate_cols = slice(br * D_MODEL + c * MXU_COLS, br * D_MODEL + (c + 1) * MXU_COLS)
            gate = _sigmoid(jnp.dot(h, wg_ref[:, gate_cols], preferred_element_type=F32))
            term = gate * jnp.dot(y_ref[...], w_ref[:, sl], preferred_element_type=F32)
            total = term if total is None else total + term
        merged[:, sl] = total.astype(BF16)
    o_ref[...] = x + jnp.dot(merged[...], wo_ref[...], preferred_element_type=F32)


def _merge(x2, norm_g, ya, yb, yc, wg, wa, wb, wc, wo):
    n = x2.shape[0]
    tm = min(MERGE_TILE, n)
    row = lambda i: (i, 0)
    const = lambda i: (0, 0)
    resident = lambda shape: pl.BlockSpec(shape, const, pipeline_mode=pl.Buffered(1))
    return pl.pallas_call(
        _merge_kernel,
        grid=(n // tm,),
        in_specs=[
            pl.BlockSpec((tm, D_MODEL), row),
            pl.BlockSpec((1, D_MODEL), const),
            pl.BlockSpec((tm, A_WIDTH), row),
            pl.BlockSpec((tm, B_WIDTH), row),
            pl.BlockSpec((tm, C_WIDTH), row),
            resident((D_MODEL, MERGE_COLS)),
            resident((A_WIDTH, D_MODEL)),
            resident((B_WIDTH, D_MODEL)),
            resident((C_WIDTH, D_MODEL)),
            resident((D_MODEL, D_MODEL)),
        ],
        out_specs=pl.BlockSpec((tm, D_MODEL), row),
        out_shape=jax.ShapeDtypeStruct((n, D_MODEL), F32),
        scratch_shapes=[pltpu.VMEM((tm, D_MODEL), BF16)],
        compiler_params=_params(("parallel",)),
        name="merge",
    )(x2, norm_g, ya, yb, yc, wg, wa, wb, wc, wo)


def _cast_kernel(w_ref, o_ref):
    o_ref[...] = w_ref[...].astype(BF16)


def _cast_bf16(w):
    depth, rows, cols = w.shape
    tr = min(CAST_ROWS, rows)
    spec = pl.BlockSpec((None, tr, cols), lambda l, i: (l, i, 0))
    return pl.pallas_call(
        _cast_kernel,
        grid=(depth, rows // tr),
        in_specs=[spec],
        out_specs=spec,
        out_shape=jax.ShapeDtypeStruct(w.shape, BF16),
        compiler_params=_params(("parallel", "parallel")),
        name="cast",
    )(w)


def kernel(x, norm_g, w_in, attn_q_norm_g, attn_k_norm_g, attn_sinks, w_o_attn, conv_w, conv_b, conv_norm_g, conv_norm_b, w_o_conv, diff_q_norm_g, diff_k_norm_g, lambda_q, lambda_k, diff_subln_g, w_o_diff, w_out):
    b, t, d = x.shape
    n = b * t
    depth = norm_g.shape[0]
    group = lax.broadcasted_iota(jnp.int32, (MXU_COLS, MXU_COLS), 0) // HEAD_DIM
    bd = (group == group.T).astype(BF16)
    tile_gain = lambda g: jnp.tile(g.astype(F32), MXU_COLS // HEAD_DIM).reshape(1, MXU_COLS)
    row = lambda v: v.astype(F32).reshape(1, -1)

    x2 = x.reshape(n, d)
    w_bf = _cast_bf16(w_in)
    for l in range(depth):
        w_k0 = w_bf[l, :, OFF_AK:OFF_AK + HEAD_DIM]
        w_k1 = w_bf[l, :, OFF_AK + HEAD_DIM:OFF_AK + A_KV_WIDTH]
        aq, ak2, avt, ag, yb, cq, ck, cvt, cg = _inproj(
            x2, b, l, row(norm_g[l]), w_bf, jnp.concatenate([w_k0, w_k1, w_k1, w_k0], axis=1),
            w_bf[l, :, OFF_AV:OFF_AV + A_KV_WIDTH].T, w_bf[l, :, OFF_CV:OFF_CV + C_WIDTH].T, bd,
            tile_gain(attn_q_norm_g[l]), tile_gain(attn_k_norm_g[l]),
            tile_gain(diff_q_norm_g[l]), tile_gain(diff_k_norm_g[l]),
            conv_w[l].astype(F32), row(conv_b[l]), row(conv_norm_g[l]), row(conv_norm_b[l]))
        r3 = lambda a: a.reshape(b, t, a.shape[-1])
        ya = _swa(attn_sinks[l].astype(F32), r3(aq), r3(ak2), avt, r3(ag))
        lam_init = 0.8 - 0.6 * math.exp(-0.3 * l)
        yc = _diff(lambda_q[l].astype(F32), lambda_k[l].astype(F32), r3(cq), r3(ck), cvt, r3(cg),
                   row(diff_subln_g[l]), lam_init)
        x2 = _merge(x2, row(norm_g[l]), ya.reshape(n, -1), yb, yc.reshape(n, -1), w_bf[l, :, OFF_MG:],
                    w_o_attn[l].astype(BF16), w_o_conv[l].astype(BF16), w_o_diff[l].astype(BF16),
                    w_out[l].astype(BF16))
    return x2.reshape(b, t, d)
```

```python
import functools
import math

import jax
import jax.numpy as jnp
from jax import lax
from jax.experimental import pallas as pl
from jax.experimental.pallas import tpu as pltpu

F32 = jnp.float32
BF16 = jnp.bfloat16

D_MODEL = 1024
HEAD_DIM = 64
EPS = 1e-6
A_Q_HEADS = 8
A_KV_HEADS = 2
A_GROUP = A_Q_HEADS // A_KV_HEADS
A_WIDTH = A_Q_HEADS * HEAD_DIM
A_KV_WIDTH = A_KV_HEADS * HEAD_DIM
WINDOW = 128
B_WIDTH = D_MODEL // 2
CONV_WIDTH = 31
C_HEADS = 4
C_HEAD_WIDTH = 2 * HEAD_DIM
C_WIDTH = C_HEADS * C_HEAD_WIDTH
N_BRANCH = 3
MERGE_COLS = N_BRANCH * D_MODEL

OFF_AQ = 0
OFF_AK = OFF_AQ + A_WIDTH
OFF_AV = OFF_AK + A_KV_WIDTH
OFF_AG = OFF_AV + A_KV_WIDTH
OFF_BGLU = OFF_AG + A_WIDTH
OFF_BG = OFF_BGLU + 2 * B_WIDTH
OFF_CQ = OFF_BG + B_WIDTH
OFF_CK = OFF_CQ + C_WIDTH
OFF_CV = OFF_CK + C_WIDTH
OFF_CG = OFF_CV + C_WIDTH
OFF_MG = OFF_CG + C_WIDTH
IN_WIDTH = OFF_MG + MERGE_COLS

MXU_COLS = 256
LANES = 128
SUBLANES = 8
VMEM_LIMIT_BYTES = 56 * 1024 * 1024

ROW_TILE = 512
MERGE_TILE = 1024
CAST_ROWS = 256
SWA_TILE = 2048
SWA_SCORE_LEAD = 4
CONV_HALO = 32
CONV_CHUNK = 64
DIFF_TQ = 512
DIFF_TK = 512
DIFF_HEADS_PER_STEP = 4
DIFF_SCORE_LEAD = 2
DIFF_SUM_ROWS = 16


def _sigmoid(y):
    return 1.0 / (1.0 + jnp.exp(-y))


def _silu(y):
    return y * _sigmoid(y)


def _params(semantics):
    return pltpu.CompilerParams(dimension_semantics=semantics, vmem_limit_bytes=VMEM_LIMIT_BYTES)


def _conv_module(ubuf, gbuf, w_ref, cb_ref, lg_ref, lb_ref, o_ref):
    tt = o_ref.shape[0]
    shifted_rows = tt + CONV_HALO - SUBLANES
    for s in range(1, SUBLANES):
        ubuf[s, 0:shifted_rows, :] = ubuf[0, s:s + shifted_rows, :]
    first = CONV_HALO - (CONV_WIDTH - 1)
    for r0 in range(0, tt, CONV_CHUNK):
        blocks = []
        for c0 in range(0, B_WIDTH, LANES):
            acc = jnp.zeros((CONV_CHUNK, LANES), F32)
            for k in range(CONV_WIDTH):
                shift = (first + k) % SUBLANES
                base = r0 + first + k - shift
                acc = acc + w_ref[k:k + 1, c0:c0 + LANES] * ubuf[shift, base:base + CONV_CHUNK, c0:c0 + LANES]
            blocks.append(acc)
        y = jnp.concatenate(blocks, axis=1) + cb_ref[...]
        mu = jnp.mean(y, axis=-1, keepdims=True)
        yc = y - mu
        var = jnp.mean(yc * yc, axis=-1, keepdims=True)
        yn = yc * lax.rsqrt(var + EPS) * lg_ref[...] + lb_ref[...]
        o_ref[r0:r0 + CONV_CHUNK, :] = (_silu(yn) * gbuf[r0:r0 + CONV_CHUNK, :]).astype(BF16)


def _inproj_kernel(x_ref, g_ref, w_ref, wak2_ref, wavt_ref, wcvt_ref, bd_ref, aqg_ref, akg_ref, cqg_ref, ckg_ref,
                   cw_ref, cb_ref, lg_ref, lb_ref,
                   aq_o, ak2_o, avt_o, ag_o, yb_o, cq_o, ck_o, cvt_o, cg_o, ubuf, gbuf, *, tiles_per_seq):
    tm = x_ref.shape[0]
    x = x_ref[...]
    ms = jnp.mean(x * x, axis=-1, keepdims=True)
    h = (x * lax.rsqrt(ms + EPS) * g_ref[...]).astype(BF16)

    def proj(off, width):
        return jnp.dot(h, w_ref[:, off:off + width], preferred_element_type=F32)

    def head_norm(y, gain, scale):
        w = y.shape[-1]
        sq = (y * y).astype(BF16)
        ssum = jnp.dot(sq, bd_ref[:w, :w], preferred_element_type=F32)
        return y * lax.rsqrt(ssum * (1.0 / HEAD_DIM) + EPS) * (gain * scale)

    qscale_log2 = HEAD_DIM ** -0.5 * math.log2(math.e)
    nt = (((1,), (1,)), ((), ()))
    chunk_cols = lambda c: slice(c * MXU_COLS, (c + 1) * MXU_COLS)
    seq_tile = pl.program_id(0) % tiles_per_seq

    @pl.when(seq_tile == 0)
    def _():
        ubuf[0, 0:CONV_HALO, :] = jnp.zeros((CONV_HALO, B_WIDTH), F32)

    @pl.when(seq_tile > 0)
    def _():
        ubuf[0, 0:CONV_HALO, :] = ubuf[0, tm:tm + CONV_HALO, :]

    for c in range(B_WIDTH // MXU_COLS):
        a = proj(OFF_BGLU + c * MXU_COLS, MXU_COLS)
        b = proj(OFF_BGLU + B_WIDTH + c * MXU_COLS, MXU_COLS)
        ubuf[0, CONV_HALO:, chunk_cols(c)] = a * _sigmoid(b)
        gbuf[:, chunk_cols(c)] = _silu(proj(OFF_BG + c * MXU_COLS, MXU_COLS))
    _conv_module(ubuf, gbuf, cw_ref, cb_ref, lg_ref, lb_ref, yb_o)

    aq = [proj(OFF_AQ + c * MXU_COLS, MXU_COLS) for c in range(A_WIDTH // MXU_COLS)]
    ak2 = jnp.dot(h, wak2_ref[...], preferred_element_type=F32)
    cq = [proj(OFF_CQ + c * MXU_COLS, MXU_COLS) for c in range(C_WIDTH // MXU_COLS)]
    ck = [proj(OFF_CK + c * MXU_COLS, MXU_COLS) for c in range(C_WIDTH // MXU_COLS)]

    for c in range(A_WIDTH // MXU_COLS):
        ag_o[:, chunk_cols(c)] = _silu(proj(OFF_AG + c * MXU_COLS, MXU_COLS)).astype(BF16)
    for c in range(C_WIDTH // MXU_COLS):
        cg_o[:, chunk_cols(c)] = _silu(proj(OFF_CG + c * MXU_COLS, MXU_COLS)).astype(BF16)
    avt_o[...] = lax.dot_general(wavt_ref[...], h, nt, preferred_element_type=F32).astype(BF16)
    cvt_o[...] = lax.dot_general(wcvt_ref[...], h, nt, preferred_element_type=F32).astype(BF16)

    for c in range(A_WIDTH // MXU_COLS):
        aq_o[:, chunk_cols(c)] = head_norm(aq[c], aqg_ref[...], qscale_log2).astype(BF16)
    ak2_o[...] = head_norm(ak2, akg_ref[...], 1.0).astype(BF16)
    for c in range(C_WIDTH // MXU_COLS):
        cq_o[:, chunk_cols(c)] = head_norm(cq[c], cqg_ref[...], qscale_log2).astype(BF16)
        ck_o[:, chunk_cols(c)] = head_norm(ck[c], ckg_ref[...], 1.0).astype(BF16)


def _inproj(x2, batch, layer, norm_g, w_in, w_ak2, w_avt, w_cvt, bd, aqg, akg, cqg, ckg, conv_w, conv_b, ln_g, ln_b):
    n = x2.shape[0]
    t = n // batch
    tm = min(ROW_TILE, t)
    per = t // tm
    const = lambda i: (0, 0)
    row = lambda i: (i, 0)
    rows = lambda w: (pl.BlockSpec((tm, w), row), jax.ShapeDtypeStruct((n, w), BF16))
    cols = lambda w: (pl.BlockSpec((None, w, tm), lambda i: (i // per, 0, i % per)),
                      jax.ShapeDtypeStruct((batch, w, t), BF16))
    outs = (rows(A_WIDTH), rows(2 * A_KV_WIDTH), cols(A_KV_WIDTH), rows(A_WIDTH), rows(B_WIDTH),
            rows(C_WIDTH), rows(C_WIDTH), cols(C_WIDTH), rows(C_WIDTH))
    resident = lambda shape: pl.BlockSpec(shape, const, pipeline_mode=pl.Buffered(1))
    return pl.pallas_call(
        functools.partial(_inproj_kernel, tiles_per_seq=per),
        grid=(n // tm,),
        in_specs=[
            pl.BlockSpec((tm, D_MODEL), row),
            pl.BlockSpec((1, D_MODEL), const),
            pl.BlockSpec((None, D_MODEL, OFF_MG), lambda i: (layer, 0, 0), pipeline_mode=pl.Buffered(1)),
            resident((D_MODEL, 2 * A_KV_WIDTH)),
            resident((A_KV_WIDTH, D_MODEL)),
            resident((C_WIDTH, D_MODEL)),
            pl.BlockSpec((MXU_COLS, MXU_COLS), const),
            pl.BlockSpec((1, MXU_COLS), const),
            pl.BlockSpec((1, MXU_COLS), const),
            pl.BlockSpec((1, MXU_COLS), const),
            pl.BlockSpec((1, MXU_COLS), const),
            pl.BlockSpec((CONV_WIDTH, B_WIDTH), const),
            pl.BlockSpec((1, B_WIDTH), const),
            pl.BlockSpec((1, B_WIDTH), const),
            pl.BlockSpec((1, B_WIDTH), const),
        ],
        out_specs=[o[0] for o in outs],
        out_shape=[o[1] for o in outs],
        scratch_shapes=[pltpu.VMEM((SUBLANES, tm + CONV_HALO, B_WIDTH), F32), pltpu.VMEM((tm, B_WIDTH), F32)],
        compiler_params=_params(("arbitrary",)),
        name="inproj",
    )(x2, norm_g, w_in, w_ak2, w_avt, w_cvt, bd, aqg, akg, cqg, ckg, conv_w, conv_b, ln_g, ln_b)


def _swa_kernel(sink_ref, q_ref, kc_ref, kp_ref, vtc_ref, vtp_ref, ag_ref, o_ref):
    i = pl.program_id(1)
    nblk = q_ref.shape[0] // WINDOW
    kfull = jnp.concatenate([kp_ref[...], kc_ref[...]], axis=0)
    vtfull = jnp.concatenate([vtp_ref[...], vtc_ref[...]], axis=1)
    pair = 2 * WINDOW
    c = lax.broadcasted_iota(jnp.int32, (pair, pair), 0)
    t = lax.broadcasted_iota(jnp.int32, (pair, pair), 1) & (WINDOW - 1)
    local = (c > t) & (c <= t + WINDOW)
    first_col = lax.broadcasted_iota(jnp.int32, (1, pair), 1) < WINDOW
    half = lax.broadcasted_iota(jnp.int32, (WINDOW, LANES), 1) < HEAD_DIM
    ones_rows = jnp.ones((DIFF_SUM_ROWS, pair), BF16)
    log2e = math.log2(math.e)
    chains = [(j, kv, par) for j in range(nblk) for kv in range(A_KV_HEADS) for par in range(2)]

    def scores(j, kv, par):
        q = q_ref[j * WINDOW:(j + 1) * WINDOW, :]
        kwin = kfull[j * WINDOW:(j + 2) * WINDOW, :]
        qz = []
        for h in (kv * A_GROUP + par, kv * A_GROUP + par + 2):
            blk = q[:, (h // 2) * LANES:(h // 2 + 1) * LANES]
            keep = half if par == 0 else jnp.logical_not(half)
            qz.append(jnp.where(keep, blk, jnp.zeros_like(blk)))
        qz = jnp.concatenate(qz, axis=0)
        ksel = kwin[:, :LANES] if par == kv else kwin[:, LANES:]
        return lax.dot_general(ksel, qz, (((1,), (1,)), ((), ())), preferred_element_type=F32)

    lead = min(SWA_SCORE_LEAD, len(chains))
    s_all = [scores(*ch) for ch in chains[:lead]]
    pieces = {}
    for n, (j, kv, par) in enumerate(chains):
        if n + lead < len(chains):
            s_all.append(scores(*chains[n + lead]))
        s = s_all[n]
        heads = (kv * A_GROUP + par, kv * A_GROUP + par + 2)
        mask = local & ((c >= WINDOW) | (i > 0)) if j == 0 else local
        s = jnp.where(mask, s, -jnp.inf)
        sink = jnp.where(first_col, sink_ref[heads[0]] * log2e, sink_ref[heads[1]] * log2e)
        m = jnp.maximum(jnp.max(s, axis=0, keepdims=True), sink)
        p = jnp.exp2(s - m).astype(BF16)
        vtwin = vtfull[kv * HEAD_DIM:(kv + 1) * HEAD_DIM, j * WINDOW:(j + 2) * WINDOW]
        acc = jnp.dot(jnp.concatenate([vtwin, ones_rows], axis=0), p, preferred_element_type=F32)
        denom = acc[HEAD_DIM:HEAD_DIM + 1, :] + jnp.exp2(sink - m)
        o = acc[:HEAD_DIM, :] * (1.0 / denom)
        pieces[j, heads[0]] = o[:, :WINDOW]
        pieces[j, heads[1]] = o[:, WINDOW:]
    for j in range(nblk):
        cols = [jnp.concatenate([pieces[j, 2 * cb], pieces[j, 2 * cb + 1]], axis=0).T for cb in range(A_Q_HEADS // 2)]
        o_all = jnp.concatenate(cols, axis=1)
        gate = ag_ref[j * WINDOW:(j + 1) * WINDOW, :].astype(F32)
        o_ref[j * WINDOW:(j + 1) * WINDOW, :] = (o_all * gate).astype(BF16)


def _swa(sinks, aq, ak2, avt, ag):
    b, t, _ = aq.shape
    tq = min(SWA_TILE, t)
    per = tq // WINDOW
    cur = lambda bi, i: (bi, i, 0)
    prev = lambda bi, i: (bi, jnp.maximum(i * per - 1, 0), 0)
    cur_t = lambda bi, i: (bi, 0, i)
    prev_t = lambda bi, i: (bi, 0, jnp.maximum(i * per - 1, 0))
    return pl.pallas_call(
        _swa_kernel,
        grid=(b, t // tq),
        in_specs=[
            pl.BlockSpec(memory_space=pltpu.SMEM),
            pl.BlockSpec((None, tq, A_WIDTH), cur),
            pl.BlockSpec((None, tq, 2 * A_KV_WIDTH), cur),
            pl.BlockSpec((None, WINDOW, 2 * A_KV_WIDTH), prev),
            pl.BlockSpec((None, A_KV_WIDTH, tq), cur_t),
            pl.BlockSpec((None, A_KV_WIDTH, WINDOW), prev_t),
            pl.BlockSpec((None, tq, A_WIDTH), cur),
        ],
        out_specs=pl.BlockSpec((None, tq, A_WIDTH), cur),
        out_shape=jax.ShapeDtypeStruct((b, t, A_WIDTH), BF16),
        compiler_params=_params(("parallel", "parallel")),
        name="swa",
    )(sinks, aq, ak2, ak2, avt, avt, ag)


def _diff_kernel(lq_ref, lk_ref, q_ref, k_ref, vt_ref, cg_ref, sg_ref, o_ref, *, lam_init):
    i = pl.program_id(2)
    tq = q_ref.shape[0]
    tk = DIFF_TK
    heads = q_ref.shape[1] // C_HEAD_WIDTH
    e = jnp.exp(jnp.sum(lq_ref[...] * lk_ref[...], axis=-1, keepdims=True))
    lam = e[0:1, :] - e[1:2, :] + lam_init

    lane = lax.broadcasted_iota(jnp.int32, (tq, C_HEAD_WIDTH), 1)
    qs = []
    for h in range(heads):
        q = q_ref[:, h * C_HEAD_WIDTH:(h + 1) * C_HEAD_WIDTH]
        zero = jnp.zeros_like(q)
        qs.append(jnp.concatenate([jnp.where(lane < HEAD_DIM, q, zero), jnp.where(lane >= HEAD_DIM, q, zero)],
                                  axis=0))

    def scores(h, start, size):
        kt = k_ref[pl.ds(start, size), h * C_HEAD_WIDTH:(h + 1) * C_HEAD_WIDTH]
        return lax.dot_general(kt, qs[h], (((1,), (1,)), ((), ())), preferred_element_type=F32)

    def head_step(h, start, size, s, carry, masked):
        m, acc = carry
        vt = vt_ref[h * C_HEAD_WIDTH:(h + 1) * C_HEAD_WIDTH, pl.ds(start, size)]
        vt = jnp.concatenate([vt, jnp.ones((DIFF_SUM_ROWS, size), BF16)], axis=0)
        if masked:
            kpos = lax.broadcasted_iota(jnp.int32, s.shape, 0)
            qpos = lax.broadcasted_iota(jnp.int32, s.shape, 1) & (tq - 1)
            s = jnp.where(kpos <= qpos, s, -jnp.inf)
        m_new = jnp.maximum(m, jnp.max(s, axis=0, keepdims=True))
        alpha = jnp.exp2(m - m_new)
        p = jnp.exp2(s - m_new).astype(BF16)
        acc = alpha * acc + jnp.dot(vt, p, preferred_element_type=F32)
        return m_new, acc

    def step(start, size, carries, masked):
        lead = min(DIFF_SCORE_LEAD, heads)
        s_all = [scores(h, start, size) for h in range(lead)]
        out = []
        for h in range(heads):
            if h + lead < heads:
                s_all.append(scores(h + lead, start, size))
            out.append(head_step(h, start, size, s_all[h], carries[h], masked))
        return tuple(out)

    init = (jnp.full((1, 2 * tq), -jnp.inf, F32), jnp.zeros((C_HEAD_WIDTH + DIFF_SUM_ROWS, 2 * tq), F32))
    carries = lax.fori_loop(
        0, i // 2, lambda jj, c: step(pl.multiple_of(jj * (2 * tk), 2 * tk), 2 * tk, c, False), (init,) * heads)
    carries = lax.cond(i % 2 == 1, lambda c: step(pl.multiple_of((i - 1) * tk, tk), tk, c, False), lambda c: c,
                       carries)
    carries = step(pl.multiple_of(i * tk, tk), tk, carries, True)
    for h in range(heads):
        _, acc = carries[h]
        o = acc[:C_HEAD_WIDTH, :] * (1.0 / acc[C_HEAD_WIDTH:C_HEAD_WIDTH + 1, :])
        d = o[:, :tq] - lam * o[:, tq:]
        ms = jnp.mean(d * d, axis=0, keepdims=True)
        y = (d * lax.rsqrt(ms + EPS)).T
        y = y * (sg_ref[...] * (1.0 - lam_init))
        sl = slice(h * C_HEAD_WIDTH, (h + 1) * C_HEAD_WIDTH)
        o_ref[:, sl] = (y * cg_ref[:, sl].astype(F32)).astype(BF16)


def _diff(lq, lk, cq, ck, cvt, cg, subln_g, lam_init):
    b, t, _ = cq.shape
    tq = min(DIFF_TQ, t)
    assert tq == DIFF_TK
    width = DIFF_HEADS_PER_STEP * C_HEAD_WIDTH
    qmap = lambda bi, h, i: (bi, i, h)
    kmap = lambda bi, h, i: (bi, 0, h)
    vtmap = lambda bi, h, i: (bi, h, 0)
    const = lambda bi, h, i: (0, 0)
    return pl.pallas_call(
        functools.partial(_diff_kernel, lam_init=lam_init),
        grid=(b, C_HEADS // DIFF_HEADS_PER_STEP, t // tq),
        in_specs=[
            pl.BlockSpec((2, HEAD_DIM), const),
            pl.BlockSpec((2, HEAD_DIM), const),
            pl.BlockSpec((None, tq, width), qmap),
            pl.BlockSpec((None, t, width), kmap),
            pl.BlockSpec((None, width, t), vtmap),
            pl.BlockSpec((None, tq, width), qmap),
            pl.BlockSpec((1, C_HEAD_WIDTH), const),
        ],
        out_specs=pl.BlockSpec((None, tq, width), qmap),
        out_shape=jax.ShapeDtypeStruct((b, t, C_WIDTH), BF16),
        compiler_params=_params(("parallel", "parallel", "arbitrary")),
        name="diff",
    )(lq, lk, cq, ck, cvt, cg, subln_g)


def _merge_kernel(x_ref, g_ref, ya_ref, yb_ref, yc_ref, wg_ref, wa_ref, wb_ref, wc_ref, wo_ref, o_ref, merged):
    x = x_ref[...]
    ms = jnp.mean(x * x, axis=-1, keepdims=True)
    h = (x * lax.rsqrt(ms + EPS) * g_ref[...]).astype(BF16)
    branches = ((ya_ref, wa_ref), (yb_ref, wb_ref), (yc_ref, wc_ref))
    for c in range(D_MODEL // MXU_COLS):
        sl = slice(c * MXU_COLS, (c + 1) * MXU_COLS)
        total = None
        for br, (y_ref, w_ref) in enumerate(branches):
            gate_cols = slice(br * D_MODEL + c * MXU_COLS, br * D_MODEL + (c + 1) * MXU_COLS)
            gate = _sigmoid(jnp.dot(h, wg_ref[:, gate_cols], preferred_element_type=F32))
            term = gate * jnp.dot(y_ref[...], w_ref[:, sl], preferred_element_type=F32)
            total = term if total is None else total + term
        merged[:, sl] = total.astype(BF16)
    o_ref[...] = x + jnp.dot(merged[...], wo_ref[...], preferred_element_type=F32)


def _merge(x2, norm_g, ya, yb, yc, wg, wa, wb, wc, wo):
    n = x2.shape[0]
    tm = min(MERGE_TILE, n)
    row = lambda i: (i, 0)
    const = lambda i: (0, 0)
    resident = lambda shape: pl.BlockSpec(shape, const, pipeline_mode=pl.Buffered(1))
    return pl.pallas_call(
        _merge_kernel,
        grid=(n // tm,),
        in_specs=[
            pl.BlockSpec((tm, D_MODEL), row),
            pl.BlockSpec((1, D_MODEL), const),
            pl.BlockSpec((tm, A_WIDTH), row),
            pl.BlockSpec((tm, B_WIDTH), row),
            pl.BlockSpec((tm, C_WIDTH), row),
            resident((D_MODEL, MERGE_COLS)),
            resident((A_WIDTH, D_MODEL)),
            resident((B_WIDTH, D_MODEL)),
            resident((C_WIDTH, D_MODEL)),
            resident((D_MODEL, D_MODEL)),
        ],
        out_specs=pl.BlockSpec((tm, D_MODEL), row),
        out_shape=jax.ShapeDtypeStruct((n, D_MODEL), F32),
        scratch_shapes=[pltpu.VMEM((tm, D_MODEL), BF16)],
        compiler_params=_params(("parallel",)),
        name="merge",
    )(x2, norm_g, ya, yb, yc, wg, wa, wb, wc, wo)


def _cast_kernel(w_ref, o_ref):
    o_ref[...] = w_ref[...].astype(BF16)


def _cast_bf16(w):
    depth, rows, cols = w.shape
    tr = min(CAST_ROWS, rows)
    spec = pl.BlockSpec((None, tr, cols), lambda l, i: (l, i, 0))
    return pl.pallas_call(
        _cast_kernel,
        grid=(depth, rows // tr),
        in_specs=[spec],
        out_specs=spec,
        out_shape=jax.ShapeDtypeStruct(w.shape, BF16),
        compiler_params=_params(("parallel", "parallel")),
        name="cast",
    )(w)


def kernel(x, norm_g, w_in, attn_q_norm_g, attn_k_norm_g, attn_sinks, w_o_attn, conv_w, conv_b, conv_norm_g, conv_norm_b, w_o_conv, diff_q_norm_g, diff_k_norm_g, lambda_q, lambda_k, diff_subln_g, w_o_diff, w_out):
    b, t, d = x.shape
    n = b * t
    depth = norm_g.shape[0]
    group = lax.broadcasted_iota(jnp.int32, (MXU_COLS, MXU_COLS), 0) // HEAD_DIM
    bd = (group == group.T).astype(BF16)
    tile_gain = lambda g: jnp.tile(g.astype(F32), MXU_COLS // HEAD_DIM).reshape(1, MXU_COLS)
    row = lambda v: v.astype(F32).reshape(1, -1)

    x2 = x.reshape(n, d)
    w_bf = _cast_bf16(w_in)
    for l in range(depth):
        w_k0 = w_bf[l, :, OFF_AK:OFF_AK + HEAD_DIM]
        w_k1 = w_bf[l, :, OFF_AK + HEAD_DIM:OFF_AK + A_KV_WIDTH]
        aq, ak2, avt, ag, yb, cq, ck, cvt, cg = _inproj(
            x2, b, l, row(norm_g[l]), w_bf, jnp.concatenate([w_k0, w_k1, w_k1, w_k0], axis=1),
            w_bf[l, :, OFF_AV:OFF_AV + A_KV_WIDTH].T, w_bf[l, :, OFF_CV:OFF_CV + C_WIDTH].T, bd,
            tile_gain(attn_q_norm_g[l]), tile_gain(attn_k_norm_g[l]),
            tile_gain(diff_q_norm_g[l]), tile_gain(diff_k_norm_g[l]),
            conv_w[l].astype(F32), row(conv_b[l]), row(conv_norm_g[l]), row(conv_norm_b[l]))
        r3 = lambda a: a.reshape(b, t, a.shape[-1])
        ya = _swa(attn_sinks[l].astype(F32), r3(aq), r3(ak2), avt, r3(ag))
        lam_init = 0.8 - 0.6 * math.exp(-0.3 * l)
        yc = _diff(lambda_q[l].astype(F32), lambda_k[l].astype(F32), r3(cq), r3(ck), cvt, r3(cg),
                   row(diff_subln_g[l]), lam_init)
        x2 = _merge(x2, row(norm_g[l]), ya.reshape(n, -1), yb, yc.reshape(n, -1), w_bf[l, :, OFF_MG:],
                    w_o_attn[l].astype(BF16), w_o_conv[l].astype(BF16), w_o_diff[l].astype(BF16),
                    w_out[l].astype(BF16))
    return x2.reshape(b, t, d)
```
